```python
import math
import jax, jax.numpy as jnp
from jax import lax
import numpy as np

D_MODEL = 2048
BATCH = 4
SEQ = 8192
DEPTH = 1
DEC_BATCH = 8
DEC_SEQ = 64
PAST_LEN = 4096

CHUNK = 64
D_MIX = D_MODEL
D_SSM = D_MIX // 2
D_CONV = D_MIX - D_SSM
SSM_GROUP = 16
N_SSM_GROUPS = D_SSM // SSM_GROUP
SSM_STATE = 64
CONV_WIDTH = 3
CONV_HEAD = 64
N_CONV_HEADS = D_CONV // CONV_HEAD
D_FF = 4 * D_MODEL
D_IN = D_SSM + 3 * D_CONV
EPS = 1e-6
STEP_MIN = 1e-3
STEP_MAX = 1e-1

kernel_name = "hybrid_s5_shortconv_stream_step"


def _rmsnorm(x, g):
    xf = x.astype(jnp.float32)
    r = lax.rsqrt(jnp.mean(xf * xf, axis=-1, keepdims=True) + EPS)
    return (xf * r).astype(x.dtype) * g


def _complex_affine_combine(earlier, later):
    a1r, a1i, b1r, b1i = earlier
    a2r, a2i, b2r, b2i = later
    ar = a2r * a1r - a2i * a1i
    ai = a2r * a1i + a2i * a1r
    br = a2r * b1r - a2i * b1i + b2r
    bi = a2r * b1i + a2i * b1r + b2i
    return (ar, ai, br, bi)


def _s5_mixer(u, h_re0, h_im0, lam_re, lam_im, log_step, b_re, b_im, c_re, c_im, d, w_glu, b_glu):
    f32 = jnp.float32
    bsz, seqlen, _ = u.shape
    uf = u.astype(f32)
    ug = uf.reshape(bsz, seqlen, N_SSM_GROUPS, SSM_GROUP)
    lre = lam_re.astype(f32)
    lim = lam_im.astype(f32)
    delta = jnp.exp(log_step.astype(f32))[:, None]
    mag = jnp.exp(lre * delta)
    a_re = mag * jnp.cos(lim * delta)
    a_im = mag * jnp.sin(lim * delta)
    den = lre * lre + lim * lim
    zr = ((a_re - 1.0) * lre + a_im * lim) / den
    zi = (a_im * lre - (a_re - 1.0) * lim) / den
    br = b_re.astype(f32)
    bi = b_im.astype(f32)
    bb_re = zr[..., None] * br - zi[..., None] * bi
    bb_im = zr[..., None] * bi + zi[..., None] * br
    cr = c_re.astype(f32)
    ci = c_im.astype(f32)

    blk = min(CHUNK, seqlen)
    nb = seqlen // blk
    u_blocks = ug.reshape(bsz, nb, blk, N_SSM_GROUPS, SSM_GROUP).transpose(1, 0, 2, 3, 4)

    def block_step(carry, ub):
        h_re, h_im = carry
        bu_re = jnp.einsum('gph,btgh->btgp', bb_re, ub)
        bu_im = jnp.einsum('gph,btgh->btgp', bb_im, ub)
        ar = jnp.broadcast_to(a_re, bu_re.shape)
        ai = jnp.broadcast_to(a_im, bu_re.shape)
        pr, pi, sr, si = lax.associative_scan(_complex_affine_combine, (ar, ai, bu_re, bu_im), axis=1)
        hr = pr * h_re[:, None] - pi * h_im[:, None] + sr
        hi = pr * h_im[:, None] + pi * h_re[:, None] + si
        y = jnp.einsum('ghp,btgp->btgh', cr, hr) - jnp.einsum('ghp,btgp->btgh', ci, hi)
        return (hr[:, -1], hi[:, -1]), y

    (h_re, h_im), y_blocks = lax.scan(block_step, (h_re0.astype(f32), h_im0.astype(f32)), u_blocks)
    y = y_blocks.transpose(1, 0, 2, 3, 4).reshape(bsz, seqlen, D_SSM)
    y = y + d.astype(f32) * uf
    y = jax.nn.gelu(y, approximate=False)
    y = y * jax.nn.sigmoid(y @ w_glu.astype(f32) + b_glu.astype(f32))
    return y.astype(u.dtype), h_re, h_im


def _layer(x, h_re0, h_im0, conv_prev, norm_mix_g, w_in, lam_re, lam_im, log_step,
           b_re, b_im, c_re, c_im, d, w_glu, b_glu, w_conv, b_conv, w_out,
           norm_mlp_g, w_up, w_down):
    seqlen = x.shape[1]
    h = _rmsnorm(x, norm_mix_g)
    proj = h @ w_in
    u = proj[..., :D_SSM]
    gate_b = proj[..., D_SSM:D_SSM + D_CONV]
    gate_c = proj[..., D_SSM + D_CONV:D_SSM + 2 * D_CONV]
    xv = proj[..., D_SSM + 2 * D_CONV:]
    y_ssm, h_re, h_im = _s5_mixer(u, h_re0, h_im0, lam_re, lam_im, log_step,
                                  b_re, b_im, c_re, c_im, d, w_glu, b_glu)
    v = gate_c * xv
    vpad = jnp.concatenate([conv_prev.astype(v.dtype), v], axis=1)
    yc = b_conv + sum(w_conv[k] * vpad[:, k:k + seqlen] for k in range(CONV_WIDTH))
    y_conv = gate_b * yc
    new_conv = vpad[:, seqlen:]
    x = x + jnp.concatenate([y_ssm, y_conv], axis=-1) @ w_out
    hm = _rmsnorm(x, norm_mlp_g) @ w_up
    x = x + jnp.square(jax.nn.relu(hm)) @ w_down
    return x, h_re, h_im, new_conv


def setup_inputs(seed: int = 0) -> dict:
    key = jax.random.key(seed)
    ks = jax.random.split(key, 26)
    f32 = jnp.float32
    nrm = lambda k, s, sc: (jax.random.normal(k, s, f32) * sc)
    G, P, H = N_SSM_GROUPS, SSM_STATE, SSM_GROUP
    lam_im_base = math.pi * jnp.arange(P, dtype=f32)
    return {
        "x_prompt": nrm(ks[0], (BATCH, SEQ, D_MODEL), 1.0),
        "x_sample": nrm(ks[1], (DEC_BATCH, DEC_SEQ, D_MODEL), 1.0),
        "state_ssm_re": nrm(ks[2], (DEPTH, DEC_BATCH, G, P), 0.1),
        "state_ssm_im": nrm(ks[3], (DEPTH, DEC_BATCH, G, P), 0.1),
        "cache_conv": nrm(ks[4], (DEPTH, DEC_BATCH, CONV_WIDTH - 1, D_CONV), 1.0),
        "norm_mix_g": 1.0 + nrm(ks[5], (DEPTH, D_MODEL), 0.02),
        "w_in": nrm(ks[6], (DEPTH, D_MODEL, D_IN), D_MODEL ** -0.5),
        "ssm_lambda_re": -0.5 * jnp.exp(nrm(ks[7], (DEPTH, G, P), 0.05)),
        "ssm_lambda_im": lam_im_base + nrm(ks[8], (DEPTH, G, P), 0.05),
        "ssm_log_step": jax.random.uniform(ks[9], (DEPTH, G), f32,
                                           math.log(STEP_MIN), math.log(STEP_MAX)),
        "ssm_b_re": nrm(ks[10], (DEPTH, G, P, H), (2 * H) ** -0.5),
        "ssm_b_im": nrm(ks[11], (DEPTH, G, P, H), (2 * H) ** -0.5),
        "ssm_c_re": nrm(ks[12], (DEPTH, G, H, P), P ** -0.5),
        "ssm_c_im": nrm(ks[13], (DEPTH, G, H, P), P ** -0.5),
        "ssm_d": nrm(ks[14], (DEPTH, D_SSM), 1.0),
        "w_glu": nrm(ks[15], (DEPTH, D_SSM, D_SSM), D_SSM ** -0.5),
        "b_glu": nrm(ks[16], (DEPTH, D_SSM), 0.01),
        "w_conv": nrm(ks[17], (DEPTH, CONV_WIDTH, D_CONV), CONV_WIDTH ** -0.5),
        "b_conv": nrm(ks[18], (DEPTH, D_CONV), 0.01),
        "w_out": nrm(ks[19], (DEPTH, D_MIX, D_MODEL), D_MIX ** -0.5),
        "norm_mlp_g": 1.0 + nrm(ks[20], (DEPTH, D_MODEL), 0.02),
        "w_up": nrm(ks[21], (DEPTH, D_MODEL, D_FF), D_MODEL ** -0.5),
        "w_down": nrm(ks[22], (DEPTH, D_FF, D_MODEL), D_FF ** -0.5),
        "norm_final_g": 1.0 + nrm(ks[23], (D_MODEL,), 0.02),
    }


def reference(x_prompt, x_sample, state_ssm_re, state_ssm_im, cache_conv,
              norm_mix_g, w_in, ssm_lambda_re, ssm_lambda_im, ssm_log_step,
              ssm_b_re, ssm_b_im, ssm_c_re, ssm_c_im, ssm_d, w_glu, b_glu,
              w_conv, b_conv, w_out, norm_mlp_g, w_up, w_down, norm_final_g):
    bp = x_prompt.shape[0]
    xp = x_prompt
    xs = x_sample
    zeros_h = jnp.zeros((bp, N_SSM_GROUPS, SSM_STATE), jnp.float32)
    zeros_conv = jnp.zeros((bp, CONV_WIDTH - 1, D_CONV), x_prompt.dtype)
    p_re, p_im, p_conv, s_re, s_im, s_conv = [], [], [], [], [], []
    for l in range(DEPTH):
        lp = (norm_mix_g[l], w_in[l], ssm_lambda_re[l], ssm_lambda_im[l], ssm_log_step[l],
              ssm_b_re[l], ssm_b_im[l], ssm_c_re[l], ssm_c_im[l], ssm_d[l], w_glu[l], b_glu[l],
              w_conv[l], b_conv[l], w_out[l], norm_mlp_g[l], w_up[l], w_down[l])
        xp, hr, hi, cv = _layer(xp, zeros_h, zeros_h, zeros_conv, *lp)
        p_re.append(hr); p_im.append(hi); p_conv.append(cv)
        xs, hr, hi, cv = _layer(xs, state_ssm_re[l], state_ssm_im[l], cache_conv[l], *lp)
        s_re.append(hr); s_im.append(hi); s_conv.append(cv)
    y_prompt = _rmsnorm(xp, norm_final_g)
    y_sample = _rmsnorm(xs, norm_final_g)
    new_ssm_re_p = jnp.stack(p_re)
    new_ssm_im_p = jnp.stack(p_im)
    new_conv_p = jnp.stack(p_conv)
    new_ssm_re_s = jnp.stack(s_re)
    new_ssm_im_s = jnp.stack(s_im)
    new_conv_s = jnp.stack(s_conv)
    return (y_prompt, y_sample, new_ssm_re_p, new_ssm_im_p, new_conv_p, new_ssm_re_s, new_ssm_im_s, new_conv_s)
```

```python
import functools

import jax
import jax.numpy as jnp
from jax import lax
from jax.experimental import pallas as pl
from jax.experimental.pallas import tpu as pltpu

D_MODEL = 2048
D_SSM = 1024
D_CONV = 1024
N_GROUPS = 64
GROUP_W = 16
N_STATE = 64
D_FF = 8192
D_IN = D_SSM + 3 * D_CONV
CONV_WIDTH = 3
EPS = 1e-6
INV_SQRT2 = 0.7071067811865476

SUBLANES = 8
LANES = 128
SEQ_PER_STEP = 4
GROUPS_PER_BLOCK = 16
N_BLOCKS = N_GROUPS // GROUPS_PER_BLOCK
BLOCK_IN = GROUPS_PER_BLOCK * GROUP_W
BLOCK_STATE = GROUPS_PER_BLOCK * N_STATE
SLABS = BLOCK_STATE // LANES
VMEM_LIMIT = 60 * 1024 * 1024


def _discretise_kernel(lre_ref, lim_ref, ls_ref, brt_ref, bit_ref, crt_ref, cit_ref,
                       are_ref, aim_ref, bblk_ref, cblk_ref, zr_ref, zi_ref):
    lre = lre_ref[...]
    lim = lim_ref[...]
    delta = jnp.exp(ls_ref[...])
    mag = jnp.exp(lre * delta)
    a_re = mag * jnp.cos(lim * delta)
    a_im = mag * jnp.sin(lim * delta)
    den = lre * lre + lim * lim
    zr_ref[...] = ((a_re - 1.0) * lre + a_im * lim) / den
    zi_ref[...] = (a_im * lre - (a_re - 1.0) * lim) / den
    are_ref[...] = a_re
    aim_ref[...] = a_im
    bblk_ref[...] = jnp.zeros(bblk_ref.shape, jnp.float32)
    cblk_ref[...] = jnp.zeros(cblk_ref.shape, jnp.float32)
    for g in range(N_GROUPS):
        j, gl = divmod(g, GROUPS_PER_BLOCK)
        zr = zr_ref[pl.ds(g, 1), :]
        zi = zi_ref[pl.ds(g, 1), :]
        br = brt_ref[g]
        bi = bit_ref[g]
        rows = slice(gl * GROUP_W, (gl + 1) * GROUP_W)
        cols = slice(gl * N_STATE, (gl + 1) * N_STATE)
        cols_im = slice(BLOCK_STATE + gl * N_STATE, BLOCK_STATE + (gl + 1) * N_STATE)
        bblk_ref[j, rows, cols] = zr * br - zi * bi
        bblk_ref[j, rows, cols_im] = zr * bi + zi * br
        cblk_ref[j, cols, rows] = crt_ref[g]
        cblk_ref[j, cols_im, rows] = -cit_ref[g]


def _discretise(lam_re, lam_im, log_step, b_re, b_im, c_re, c_im):
    f32 = jnp.float32
    brt = jnp.transpose(b_re, (0, 2, 1))
    bit = jnp.transpose(b_im, (0, 2, 1))
    crt = jnp.transpose(c_re, (0, 2, 1))
    cit = jnp.transpose(c_im, (0, 2, 1))
    out_shape = (
        jax.ShapeDtypeStruct((N_GROUPS, N_STATE), f32),
        jax.ShapeDtypeStruct((N_GROUPS, N_STATE), f32),
        jax.ShapeDtypeStruct((N_BLOCKS, BLOCK_IN, 2 * BLOCK_STATE), f32),
        jax.ShapeDtypeStruct((N_BLOCKS, 2 * BLOCK_STATE, BLOCK_IN), f32),
    )
    return pl.pallas_call(
        _discretise_kernel,
        out_shape=out_shape,
        scratch_shapes=[pltpu.VMEM((N_GROUPS, N_STATE), f32), pltpu.VMEM((N_GROUPS, N_STATE), f32)],
        compiler_params=pltpu.CompilerParams(vmem_limit_bytes=VMEM_LIMIT),
        name="discretise",
    )(lam_re, lam_im, log_step.reshape(N_GROUPS, 1), brt, bit, crt, cit)


def _mixer_kernel(x_ref, h0_ref, c0_ref, g_ref, win_ref, a_ref, bblk_ref, cblk_ref, d_ref,
                  wglu_ref, bglu_ref, wconv_ref, bconv_ref,
                  mix_ref, hout_ref, cout_ref,
                  bu_ref, uperm_ref, yperm_ref, vbuf_ref, *, tt):
    f32, bf16 = jnp.float32, jnp.bfloat16
    nseq = SEQ_PER_STEP
    m = nseq * tt
    nth = tt // SUBLANES
    ti = pl.program_id(1)

    @pl.when(ti == 0)
    def _():
        hout_ref[...] = h0_ref[...]
        vbuf_ref[:, 0:SUBLANES, :] = c0_ref[...]

    x = x_ref[...].reshape(m, D_MODEL)
    r = lax.rsqrt(jnp.mean(x * x, axis=-1, keepdims=True) + EPS)
    xn = ((x * r) * g_ref[...]).astype(bf16)

    u = jnp.dot(xn, win_ref[:, 0:D_SSM], preferred_element_type=f32)
    gate_b = jnp.dot(xn, win_ref[:, D_SSM:D_SSM + D_CONV], preferred_element_type=f32)
    gate_c = jnp.dot(xn, win_ref[:, D_SSM + D_CONV:D_SSM + 2 * D_CONV], preferred_element_type=f32)
    xv = jnp.dot(xn, win_ref[:, D_SSM + 2 * D_CONV:], preferred_element_type=f32)

    v = gate_c * xv
    vbuf_ref[:, SUBLANES:, :] = v.reshape(nseq, tt, D_CONV)
    v1 = vbuf_ref[:, SUBLANES - 1:SUBLANES - 1 + tt, :].reshape(m, D_CONV)
    v2 = vbuf_ref[:, SUBLANES - 2:SUBLANES - 2 + tt, :].reshape(m, D_CONV)
    yc = bconv_ref[...] + wconv_ref[0:1, :] * v2 + wconv_ref[1:2, :] * v1 + wconv_ref[2:3, :] * v
    y_conv = gate_b * yc
    tail = vbuf_ref[:, tt:tt + SUBLANES, :]
    vbuf_ref[:, 0:SUBLANES, :] = tail
    cout_ref[...] = tail

    u4 = u.reshape(nseq, nth, SUBLANES, D_SSM)
    for b in range(nseq):
        uperm_ref[:, b] = u4[b]
    up = uperm_ref[...].reshape(m, D_SSM).astype(bf16)

    for j in range(N_BLOCKS):
        bu = jnp.dot(up[:, j * BLOCK_IN:(j + 1) * BLOCK_IN], bblk_ref[j], preferred_element_type=f32)
        for k in range(2 * SLABS):
            bu_ref[k] = bu[:, k * LANES:(k + 1) * LANES]
        a_re = [a_ref[j * 2 * SLABS + k, 0:nseq, :] for k in range(SLABS)]
        a_im = [a_ref[j * 2 * SLABS + SLABS + k, 0:nseq, :] for k in range(SLABS)]

        def step8(th, carry, a_re=a_re, a_im=a_im):
            hs = list(carry)
            base = th * (nseq * SUBLANES)
            for tl in range(SUBLANES):
                rows = pl.ds(base + tl, nseq, stride=SUBLANES)
                for k in range(SLABS):
                    hr, hi = hs[k], hs[SLABS + k]
                    nr = a_re[k] * hr - a_im[k] * hi + bu_ref[k, rows, :]
                    ni = a_re[k] * hi + a_im[k] * hr + bu_ref[SLABS + k, rows, :]
                    bu_ref[k, rows, :] = nr
                    bu_ref[SLABS + k, rows, :] = ni
                    hs[k], hs[SLABS + k] = nr, ni
            return tuple(hs)

        init = tuple(hout_ref[j * 2 * SLABS + k] for k in range(2 * SLABS))
        fin = lax.fori_loop(0, nth, step8, init)
        for k in range(2 * SLABS):
            hout_ref[j * 2 * SLABS + k] = fin[k]
        h_all = jnp.concatenate([bu_ref[k] for k in range(2 * SLABS)], axis=1).astype(bf16)
        yj = jnp.dot(h_all, cblk_ref[j], preferred_element_type=f32)
        yperm_ref[:, :, :, j * BLOCK_IN:(j + 1) * BLOCK_IN] = yj.reshape(nth, nseq, SUBLANES, BLOCK_IN)

    y = jnp.concatenate([yperm_ref[:, b].reshape(tt, D_SSM) for b in range(nseq)], axis=0)
    y = y + d_ref[...] * u
    y = 0.5 * y * (1.0 + lax.erf(y * INV_SQRT2))
    gl = jnp.dot(y.astype(bf16), wglu_ref[...], preferred_element_type=f32) + bglu_ref[...]
    y_ssm = y * jax.nn.sigmoid(gl)

    mix = jnp.concatenate([y_ssm, y_conv], axis=-1).astype(bf16)
    mix_ref[...] = mix.reshape(nseq, tt, D_MODEL)


def _const_spec(shape):
    zeros = (0,) * len(shape)
    return pl.BlockSpec(shape, lambda *_: zeros, pipeline_mode=pl.Buffered(1))


def _mixer(x, h0, c0, norm_g, w_in, a_tab, bblk, cblk, d, w_glu, b_glu, w_conv, b_conv, *, tt):
    nbg, nseq, seqlen, _ = x.shape
    assert nseq == SEQ_PER_STEP and seqlen % tt == 0 and tt % SUBLANES == 0
    nt = seqlen // tt
    m = nseq * tt
    f32 = jnp.float32
    nstate_rows = N_BLOCKS * 2 * SLABS
    in_specs = [
        pl.BlockSpec((None, nseq, tt, D_MODEL), lambda bg, ti: (bg, 0, ti, 0)),
        pl.BlockSpec((None, nstate_rows, nseq, LANES), lambda bg, ti: (bg, 0, 0, 0)),
        pl.BlockSpec((None, nseq, SUBLANES, D_CONV), lambda bg, ti: (bg, 0, 0, 0)),
        _const_spec((1, D_MODEL)),
        _const_spec((D_MODEL, D_IN)),
        _const_spec((nstate_rows, SUBLANES, LANES)),
        _const_spec((N_BLOCKS, BLOCK_IN, 2 * BLOCK_STATE)),
        _const_spec((N_BLOCKS, 2 * BLOCK_STATE, BLOCK_IN)),
        _const_spec((1, D_SSM)),
        _const_spec((D_SSM, D_SSM)),
        _const_spec((1, D_SSM)),
        _const_spec((CONV_WIDTH, D_CONV)),
        _const_spec((1, D_CONV)),
    ]
    out_specs = [
        pl.BlockSpec((None, nseq, tt, D_MODEL), lambda bg, ti: (bg, 0, ti, 0)),
        pl.BlockSpec((None, nstate_rows, nseq, LANES), lambda bg, ti: (bg, 0, 0, 0)),
        pl.BlockSpec((None, nseq, SUBLANES, D_CONV), lambda bg, ti: (bg, 0, 0, 0)),
    ]
    out_shape = (
        jax.ShapeDtypeStruct((nbg, nseq, seqlen, D_MODEL), jnp.bfloat16),
        jax.ShapeDtypeStruct((nbg, nstate_rows, nseq, LANES), f32),
        jax.ShapeDtypeStruct((nbg, nseq, SUBLANES, D_CONV), f32),
    )
    scratch = [
        pltpu.VMEM((2 * SLABS, m, LANES), f32),
        pltpu.VMEM((tt // SUBLANES, nseq, SUBLANES, D_SSM), f32),
        pltpu.VMEM((tt // SUBLANES, nseq, SUBLANES, D_SSM), f32),
        pltpu.VMEM((nseq, tt + SUBLANES, D_CONV), f32),
    ]
    return pl.pallas_call(
        functools.partial(_mixer_kernel, tt=tt),
        grid=(nbg, nt),
        in_specs=in_specs,
        out_specs=out_specs,
        out_shape=out_shape,
        scratch_shapes=scratch,
        compiler_params=pltpu.CompilerParams(
            dimension_semantics=("arbitrary", "arbitrary"), vmem_limit_bytes=VMEM_LIMIT),
        name="mixer",
    )(x, h0, c0, norm_g, w_in, a_tab, bblk, cblk, d, w_glu, b_glu, w_conv, b_conv)


def _mlp_kernel(x_ref, mix_ref, wout_ref, g_ref, wup_ref, wdown_ref, gfin_ref, o_ref, xn_ref):
    f32, bf16 = jnp.float32, jnp.bfloat16
    f = pl.program_id(1)

    @pl.when(f == 0)
    def _():
        x1 = x_ref[...] + jnp.dot(mix_ref[...], wout_ref[...], preferred_element_type=f32)
        o_ref[...] = x1
        r = lax.rsqrt(jnp.mean(x1 * x1, axis=-1, keepdims=True) + EPS)
        xn_ref[...] = ((x1 * r) * g_ref[...]).astype(bf16)

    hm = jnp.dot(xn_ref[...], wup_ref[...], preferred_element_type=f32)
    act = jnp.square(jnp.maximum(hm, 0.0)).astype(bf16)
    o_ref[...] += jnp.dot(act, wdown_ref[...], preferred_element_type=f32)

    @pl.when(f == pl.num_programs(1) - 1)
    def _():
        x2 = o_ref[...]
        r = lax.rsqrt(jnp.mean(x2 * x2, axis=-1, keepdims=True) + EPS)
        o_ref[...] = (x2 * r) * gfin_ref[...]


def _mlp(x, mix, w_out, norm_g, w_up, w_down, norm_final_g, *, tm, tf):
    tokens = x.shape[0]
    assert tokens % tm == 0 and D_FF % tf == 0
    in_specs = [
        pl.BlockSpec((tm, D_MODEL), lambda i, f: (i, 0)),
        pl.BlockSpec((tm, D_MODEL), lambda i, f: (i, 0)),
        _const_spec((D_MODEL, D_MODEL)),
        _const_spec((1, D_MODEL)),
        pl.BlockSpec((D_MODEL, tf), lambda i, f: (0, f)),
        pl.BlockSpec((tf, D_MODEL), lambda i, f: (f, 0)),
        _const_spec((1, D_MODEL)),
    ]
    return pl.pallas_call(
        _mlp_kernel,
        grid=(tokens // tm, D_FF // tf),
        in_specs=in_specs,
        out_specs=pl.BlockSpec((tm, D_MODEL), lambda i, f: (i, 0)),
        out_shape=jax.ShapeDtypeStruct((tokens, D_MODEL), jnp.float32),
        scratch_shapes=[pltpu.VMEM((tm, D_MODEL), jnp.bfloat16)],
        compiler_params=pltpu.CompilerParams(
            dimension_semantics=("arbitrary", "arbitrary"), vmem_limit_bytes=VMEM_LIMIT),
        name="mlp",
    )(x, mix, w_out, norm_g, w_up, w_down, norm_final_g)


def _state_to_rows(h_re, h_im):
    def one(h):
        nb = h.shape[0]
        h = h.reshape(nb // SEQ_PER_STEP, SEQ_PER_STEP, N_BLOCKS, SLABS, LANES)
        return jnp.transpose(h, (0, 2, 3, 1, 4))
    both = jnp.stack([one(h_re), one(h_im)], axis=2)
    return both.reshape(both.shape[0], N_BLOCKS * 2 * SLABS, SEQ_PER_STEP, LANES)


def _rows_to_state(rows):
    nbg = rows.shape[0]
    r = rows.reshape(nbg, N_BLOCKS, 2, SLABS, SEQ_PER_STEP, LANES)
    r = jnp.transpose(r, (2, 0, 4, 1, 3, 5))
    r = r.reshape(2, nbg * SEQ_PER_STEP, N_GROUPS, N_STATE)
    return r[0], r[1]


def _a_table(a_re, a_im):
    tab = jnp.stack([a_re.reshape(N_BLOCKS, SLABS, LANES), a_im.reshape(N_BLOCKS, SLABS, LANES)], axis=1)
    tab = tab.reshape(N_BLOCKS * 2 * SLABS, 1, LANES)
    return jnp.broadcast_to(tab, (N_BLOCKS * 2 * SLABS, SUBLANES, LANES))


def _run_stream(x, h_re, h_im, conv_prev, lw, a_tab, bblk, cblk, norm_final_g, *, tt, tm, tf):
    bsz, seqlen, _ = x.shape
    nbg = bsz // SEQ_PER_STEP
    x4 = x.reshape(nbg, SEQ_PER_STEP, seqlen, D_MODEL)
    h0 = _state_to_rows(h_re, h_im)
    c0 = jnp.pad(conv_prev, ((0, 0), (SUBLANES - (CONV_WIDTH - 1), 0), (0, 0)))
    c0 = c0.reshape(nbg, SEQ_PER_STEP, SUBLANES, D_CONV)
    mix, hout, cout = _mixer(x4, h0, c0, lw["norm_mix_g"], lw["w_in"], a_tab, bblk, cblk, lw["d"],
                             lw["w_glu"], lw["b_glu"], lw["w_conv"], lw["b_conv"], tt=tt)
    y = _mlp(x.reshape(bsz * seqlen, D_MODEL), mix.reshape(bsz * seqlen, D_MODEL), lw["w_out"],
             lw["norm_mlp_g"], lw["w_up"], lw["w_down"], norm_final_g, tm=tm, tf=tf)
    new_re, new_im = _rows_to_state(hout)
    new_conv = cout.reshape(bsz, SUBLANES, D_CONV)[:, SUBLANES - (CONV_WIDTH - 1):, :]
    return y.reshape(bsz, seqlen, D_MODEL), new_re, new_im, new_conv


def kernel(x_prompt, x_sample, state_ssm_re, state_ssm_im, cache_conv, norm_mix_g, w_in, ssm_lambda_re, ssm_lambda_im, ssm_log_step, ssm_b_re, ssm_b_im, ssm_c_re, ssm_c_im, ssm_d, w_glu, b_glu, w_conv, b_conv, w_out, norm_mlp_g, w_up, w_down, norm_final_g):
    depth = w_in.shape[0]
    assert depth == 1, "the final norm is fused into the last (only) layer"
    bf16 = jnp.bfloat16
    bp = x_prompt.shape[0]
    a_re, a_im, bblk, cblk = _discretise(ssm_lambda_re[0], ssm_lambda_im[0], ssm_log_step[0],
                                         ssm_b_re[0], ssm_b_im[0], ssm_c_re[0], ssm_c_im[0])
    a_tab = _a_table(a_re, a_im)
    lw = dict(
        norm_mix_g=norm_mix_g[0].reshape(1, D_MODEL), w_in=w_in[0].astype(bf16),
        d=ssm_d[0].reshape(1, D_SSM), w_glu=w_glu[0].astype(bf16), b_glu=b_glu[0].reshape(1, D_SSM),
        w_conv=w_conv[0], b_conv=b_conv[0].reshape(1, D_CONV), w_out=w_out[0].astype(bf16),
        norm_mlp_g=norm_mlp_g[0].reshape(1, D_MODEL), w_up=w_up[0].astype(bf16),
        w_down=w_down[0].astype(bf16))
    gfin = norm_final_g.reshape(1, D_MODEL)
    bblk = bblk.astype(bf16)
    cblk = cblk.astype(bf16)

    zeros_h = jnp.zeros((bp, N_GROUPS, N_STATE), jnp.float32)
    zeros_c = jnp.zeros((bp, CONV_WIDTH - 1, D_CONV), x_prompt.dtype)
    yp, pre, pim, pcv = _run_stream(x_prompt, zeros_h, zeros_h, zeros_c, lw, a_tab, bblk, cblk, gfin,
                                    tt=128, tm=512, tf=1024)
    ys, sre, sim, scv = _run_stream(x_sample, state_ssm_re[0], state_ssm_im[0], cache_conv[0], lw, a_tab,
                                    bblk, cblk, gfin, tt=64, tm=512, tf=1024)
    return (yp, ys, pre[None], pim[None], pcv[None], sre[None], sim[None], scv[None])
```

```python
import functools

import jax
import jax.numpy as jnp
from jax import lax
from jax.experimental import pallas as pl
from jax.experimental.pallas import tpu as pltpu

D_MODEL = 2048
D_SSM = 1024
D_CONV = 1024
N_GROUPS = 64
GROUP_W = 16
N_STATE = 64
D_FF = 8192
D_IN = D_SSM + 3 * D_CONV
CONV_WIDTH = 3
EPS = 1e-6
INV_SQRT2 = 0.7071067811865476

SUBLANES = 8
LANES = 128
SEQ_PER_STEP = 4
GROUPS_PER_BLOCK = 16
N_BLOCKS = N_GROUPS // GROUPS_PER_BLOCK
BLOCK_IN = GROUPS_PER_BLOCK * GROUP_W
BLOCK_STATE = GROUPS_PER_BLOCK * N_STATE
SLABS = BLOCK_STATE // LANES
GATE_CHUNK = 256
N_GATE_CHUNKS = 3 * D_CONV // GATE_CHUNK
VMEM_LIMIT = 60 * 1024 * 1024


def _discretise_kernel(lre_ref, lim_ref, ls_ref, brt_ref, bit_ref, crt_ref, cit_ref,
                       are_ref, aim_ref, bblk_ref, cblk_ref, zr_ref, zi_ref):
    lre = lre_ref[...]
    lim = lim_ref[...]
    delta = jnp.exp(ls_ref[...])
    mag = jnp.exp(lre * delta)
    a_re = mag * jnp.cos(lim * delta)
    a_im = mag * jnp.sin(lim * delta)
    den = lre * lre + lim * lim
    zr_ref[...] = ((a_re - 1.0) * lre + a_im * lim) / den
    zi_ref[...] = (a_im * lre - (a_re - 1.0) * lim) / den
    are_ref[...] = a_re
    aim_ref[...] = a_im
    bblk_ref[...] = jnp.zeros(bblk_ref.shape, jnp.float32)
    cblk_ref[...] = jnp.zeros(cblk_ref.shape, jnp.float32)
    for g in range(N_GROUPS):
        j, gl = divmod(g, GROUPS_PER_BLOCK)
        zr = zr_ref[pl.ds(g, 1), :]
        zi = zi_ref[pl.ds(g, 1), :]
        br = brt_ref[g]
        bi = bit_ref[g]
        rows = slice(gl * GROUP_W, (gl + 1) * GROUP_W)
        cols = slice(gl * N_STATE, (gl + 1) * N_STATE)
        cols_im = slice(BLOCK_STATE + gl * N_STATE, BLOCK_STATE + (gl + 1) * N_STATE)
        bblk_ref[j, rows, cols] = zr * br - zi * bi
        bblk_ref[j, rows, cols_im] = zr * bi + zi * br
        cblk_ref[j, cols, rows] = crt_ref[g]
        cblk_ref[j, cols_im, rows] = -cit_ref[g]


def _discretise(lam_re, lam_im, log_step, b_re, b_im, c_re, c_im):
    f32 = jnp.float32
    brt = jnp.transpose(b_re, (0, 2, 1))
    bit = jnp.transpose(b_im, (0, 2, 1))
    crt = jnp.transpose(c_re, (0, 2, 1))
    cit = jnp.transpose(c_im, (0, 2, 1))
    out_shape = (
        jax.ShapeDtypeStruct((N_GROUPS, N_STATE), f32),
        jax.ShapeDtypeStruct((N_GROUPS, N_STATE), f32),
        jax.ShapeDtypeStruct((N_BLOCKS, BLOCK_IN, 2 * BLOCK_STATE), f32),
        jax.ShapeDtypeStruct((N_BLOCKS, 2 * BLOCK_STATE, BLOCK_IN), f32),
    )
    return pl.pallas_call(
        _discretise_kernel,
        out_shape=out_shape,
        scratch_shapes=[pltpu.VMEM((N_GROUPS, N_STATE), f32), pltpu.VMEM((N_GROUPS, N_STATE), f32)],
        compiler_params=pltpu.CompilerParams(vmem_limit_bytes=VMEM_LIMIT),
        name="discretise",
    )(lam_re, lam_im, log_step.reshape(N_GROUPS, 1), brt, bit, crt, cit)


def _mixer_kernel(x_ref, h0_ref, c0_ref, g_ref, wu_ref, wg_ref, a_ref, bblk_ref, cblk_ref, d_ref,
                  wglu_ref, bglu_ref, wconv_ref, bconv_ref,
                  mix_ref, hout_ref, cout_ref,
                  bu_ref, perm_ref, vbuf_ref, xn_ref, gates_ref, *, tt):
    f32, bf16 = jnp.float32, jnp.bfloat16
    nseq = SEQ_PER_STEP
    m = nseq * tt
    nth = tt // SUBLANES
    ti = pl.program_id(1)

    @pl.when(ti == 0)
    def _():
        hout_ref[...] = h0_ref[...]
        vbuf_ref[:, 0:SUBLANES, :] = c0_ref[...]

    x = x_ref[...].reshape(m, D_MODEL)
    r = lax.rsqrt(jnp.mean(x * x, axis=-1, keepdims=True) + EPS)
    xn_ref[...] = ((x * r) * g_ref[...]).astype(bf16)

    u = jnp.dot(xn_ref[...], wu_ref[...], preferred_element_type=f32)

    u4 = u.reshape(nseq, nth, SUBLANES, D_SSM)
    for b in range(nseq):
        perm_ref[:, b] = u4[b]
    up = perm_ref[...].reshape(m, D_SSM).astype(bf16)

    trips = N_GATE_CHUNKS // (N_BLOCKS - 1)
    steps = tt // trips
    for j in range(N_BLOCKS):
        bu = jnp.dot(up[:, j * BLOCK_IN:(j + 1) * BLOCK_IN], bblk_ref[j], preferred_element_type=f32)
        for k in range(2 * SLABS):
            bu_ref[k] = bu[:, k * LANES:(k + 1) * LANES]
        a_re = [a_ref[j * 2 * SLABS + k, 0:nseq, :] for k in range(SLABS)]
        a_im = [a_ref[j * 2 * SLABS + SLABS + k, 0:nseq, :] for k in range(SLABS)]

        def trip(q, carry, j=j, a_re=a_re, a_im=a_im):
            if j < N_BLOCKS - 1:
                c = j * trips + q
                gates_ref[c] = jnp.dot(xn_ref[...], wg_ref[c], preferred_element_type=f32)
            hs = list(carry)
            for s in range(steps):
                row0 = q * (steps * nseq) + (s // SUBLANES) * (nseq * SUBLANES) + s % SUBLANES
                rows = pl.ds(row0, nseq, stride=SUBLANES)
                for k in range(SLABS):
                    hr, hi = hs[k], hs[SLABS + k]
                    nr = a_re[k] * hr - a_im[k] * hi + bu_ref[k, rows, :]
                    ni = a_re[k] * hi + a_im[k] * hr + bu_ref[SLABS + k, rows, :]
                    bu_ref[k, rows, :] = nr
                    bu_ref[SLABS + k, rows, :] = ni
                    hs[k], hs[SLABS + k] = nr, ni
            return tuple(hs)

        init = tuple(hout_ref[j * 2 * SLABS + k] for k in range(2 * SLABS))
        fin = init
        for q in range(trips):
            fin = trip(q, fin)
        for k in range(2 * SLABS):
            hout_ref[j * 2 * SLABS + k] = fin[k]
        h_all = jnp.concatenate([bu_ref[k] for k in range(2 * SLABS)], axis=1).astype(bf16)
        yj = jnp.dot(h_all, cblk_ref[j], preferred_element_type=f32)
        cols = slice(j * BLOCK_IN, (j + 1) * BLOCK_IN)
        perm_ref[:, :, :, cols] = (yj.reshape(nth, nseq, SUBLANES, BLOCK_IN)
                                   + d_ref[:, cols] * perm_ref[:, :, :, cols])

    per_gate = D_CONV // GATE_CHUNK
    for c in range(per_gate):
        cols = slice(c * GATE_CHUNK, (c + 1) * GATE_CHUNK)
        v = gates_ref[per_gate + c] * gates_ref[2 * per_gate + c]
        vbuf_ref[:, SUBLANES:, cols] = v.reshape(nseq, tt, GATE_CHUNK)
        v1 = vbuf_ref[:, SUBLANES - 1:SUBLANES - 1 + tt, cols].reshape(m, GATE_CHUNK)
        v2 = vbuf_ref[:, SUBLANES - 2:SUBLANES - 2 + tt, cols].reshape(m, GATE_CHUNK)
        yc = (bconv_ref[:, cols] + wconv_ref[0:1, cols] * v2 + wconv_ref[1:2, cols] * v1
              + wconv_ref[2:3, cols] * v)
        y_conv = gates_ref[c] * yc
        mix_ref[:, :, D_SSM + c * GATE_CHUNK:D_SSM + (c + 1) * GATE_CHUNK] = (
            y_conv.astype(bf16).reshape(nseq, tt, GATE_CHUNK))
    tail = vbuf_ref[:, tt:tt + SUBLANES, :]
    vbuf_ref[:, 0:SUBLANES, :] = tail
    cout_ref[...] = tail

    y = jnp.concatenate([perm_ref[:, b].reshape(tt, D_SSM) for b in range(nseq)], axis=0)
    y = 0.5 * y * (1.0 + lax.erf(y * INV_SQRT2))
    gl = jnp.dot(y.astype(bf16), wglu_ref[...], preferred_element_type=f32) + bglu_ref[...]
    y_ssm = y * jax.nn.sigmoid(gl)
    mix_ref[:, :, 0:D_SSM] = y_ssm.astype(bf16).reshape(nseq, tt, D_SSM)


def _const_spec(shape):
    zeros = (0,) * len(shape)
    return pl.BlockSpec(shape, lambda *_: zeros, pipeline_mode=pl.Buffered(1))


def _mixer(x, h0, c0, norm_g, w_u, w_gates, a_tab, bblk, cblk, d, w_glu, b_glu, w_conv, b_conv, *, tt):
    nbg, nseq, seqlen, _ = x.shape
    assert nseq == SEQ_PER_STEP and seqlen % tt == 0 and tt % SUBLANES == 0
    nt = seqlen // tt
    m = nseq * tt
    f32 = jnp.float32
    nstate_rows = N_BLOCKS * 2 * SLABS
    in_specs = [
        pl.BlockSpec((None, nseq, tt, D_MODEL), lambda bg, ti: (bg, 0, ti, 0)),
        pl.BlockSpec((None, nstate_rows, nseq, LANES), lambda bg, ti: (bg, 0, 0, 0)),
        pl.BlockSpec((None, nseq, SUBLANES, D_CONV), lambda bg, ti: (bg, 0, 0, 0)),
        _const_spec((1, D_MODEL)),
        _const_spec((D_MODEL, D_SSM)),
        _const_spec((N_GATE_CHUNKS, D_MODEL, GATE_CHUNK)),
        _const_spec((nstate_rows, SUBLANES, LANES)),
        _const_spec((N_BLOCKS, BLOCK_IN, 2 * BLOCK_STATE)),
        _const_spec((N_BLOCKS, 2 * BLOCK_STATE, BLOCK_IN)),
        _const_spec((1, D_SSM)),
        _const_spec((D_SSM, D_SSM)),
        _const_spec((1, D_SSM)),
        _const_spec((CONV_WIDTH, D_CONV)),
        _const_spec((1, D_CONV)),
    ]
    out_specs = [
        pl.BlockSpec((None, nseq, tt, D_MODEL), lambda bg, ti: (bg, 0, ti, 0)),
        pl.BlockSpec((None, nstate_rows, nseq, LANES), lambda bg, ti: (bg, 0, 0, 0)),
        pl.BlockSpec((None, nseq, SUBLANES, D_CONV), lambda bg, ti: (bg, 0, 0, 0)),
    ]
    out_shape = (
        jax.ShapeDtypeStruct((nbg, nseq, seqlen, D_MODEL), jnp.bfloat16),
        jax.ShapeDtypeStruct((nbg, nstate_rows, nseq, LANES), f32),
        jax.ShapeDtypeStruct((nbg, nseq, SUBLANES, D_CONV), f32),
    )
    scratch = [
        pltpu.VMEM((2 * SLABS, m, LANES), f32),
        pltpu.VMEM((tt // SUBLANES, nseq, SUBLANES, D_SSM), f32),
        pltpu.VMEM((nseq, tt + SUBLANES, D_CONV), f32),
        pltpu.VMEM((m, D_MODEL), jnp.bfloat16),
        pltpu.VMEM((N_GATE_CHUNKS, m, GATE_CHUNK), f32),
    ]
    return pl.pallas_call(
        functools.partial(_mixer_kernel, tt=tt),
        grid=(nbg, nt),
        in_specs=in_specs,
        out_specs=out_specs,
        out_shape=out_shape,
        scratch_shapes=scratch,
        compiler_params=pltpu.CompilerParams(
            dimension_semantics=("arbitrary", "arbitrary"), vmem_limit_bytes=VMEM_LIMIT),
        name="mixer",
    )(x, h0, c0, norm_g, w_u, w_gates, a_tab, bblk, cblk, d, w_glu, b_glu, w_conv, b_conv)


def _mlp_kernel(x_ref, mix_ref, wout_ref, g_ref, wup_ref, wdown_ref, gfin_ref, o_ref, xn_ref):
    f32, bf16 = jnp.float32, jnp.bfloat16
    f = pl.program_id(1)

    @pl.when(f == 0)
    def _():
        x1 = x_ref[...] + jnp.dot(mix_ref[...], wout_ref[...], preferred_element_type=f32)
        o_ref[...] = x1
        r = lax.rsqrt(jnp.mean(x1 * x1, axis=-1, keepdims=True) + EPS)
        xn_ref[...] = ((x1 * r) * g_ref[...]).astype(bf16)

    hm = jnp.dot(xn_ref[...], wup_ref[...], preferred_element_type=f32)
    act = jnp.square(jnp.maximum(hm, 0.0)).astype(bf16)
    o_ref[...] += jnp.dot(act, wdown_ref[...], preferred_element_type=f32)

    @pl.when(f == pl.num_programs(1) - 1)
    def _():
        x2 = o_ref[...]
        r = lax.rsqrt(jnp.mean(x2 * x2, axis=-1, keepdims=True) + EPS)
        o_ref[...] = (x2 * r) * gfin_ref[...]


def _mlp(x, mix, w_out, norm_g, w_up, w_down, norm_final_g, *, tm, tf):
    tokens = x.shape[0]
    assert tokens % tm == 0 and D_FF % tf == 0
    in_specs = [
        pl.BlockSpec((tm, D_MODEL), lambda i, f: (i, 0)),
        pl.BlockSpec((tm, D_MODEL), lambda i, f: (i, 0)),
        _const_spec((D_MODEL, D_MODEL)),
        _const_spec((1, D_MODEL)),
        pl.BlockSpec((D_MODEL, tf), lambda i, f: (0, f)),
        pl.BlockSpec((tf, D_MODEL), lambda i, f: (f, 0)),
        _const_spec((1, D_MODEL)),
    ]
    return pl.pallas_call(
        _mlp_kernel,
        grid=(tokens // tm, D_FF // tf),
        in_specs=in_specs,
        out_specs=pl.BlockSpec((tm, D_MODEL), lambda i, f: (i, 0)),
        out_shape=jax.ShapeDtypeStruct((tokens, D_MODEL), jnp.float32),
        scratch_shapes=[pltpu.VMEM((tm, D_MODEL), jnp.bfloat16)],
        compiler_params=pltpu.CompilerParams(
            dimension_semantics=("arbitrary", "arbitrary"), vmem_limit_bytes=VMEM_LIMIT),
        name="mlp",
    )(x, mix, w_out, norm_g, w_up, w_down, norm_final_g)


def _state_to_rows(h_re, h_im):
    def one(h):
        nb = h.shape[0]
        h = h.reshape(nb // SEQ_PER_STEP, SEQ_PER_STEP, N_BLOCKS, SLABS, LANES)
        return jnp.transpose(h, (0, 2, 3, 1, 4))
    both = jnp.stack([one(h_re), one(h_im)], axis=2)
    return both.reshape(both.shape[0], N_BLOCKS * 2 * SLABS, SEQ_PER_STEP, LANES)


def _rows_to_state(rows):
    nbg = rows.shape[0]
    r = rows.reshape(nbg, N_BLOCKS, 2, SLABS, SEQ_PER_STEP, LANES)
    r = jnp.transpose(r, (2, 0, 4, 1, 3, 5))
    r = r.reshape(2, nbg * SEQ_PER_STEP, N_GROUPS, N_STATE)
    return r[0], r[1]


def _a_table(a_re, a_im):
    tab = jnp.stack([a_re.reshape(N_BLOCKS, SLABS, LANES), a_im.reshape(N_BLOCKS, SLABS, LANES)], axis=1)
    tab = tab.reshape(N_BLOCKS * 2 * SLABS, 1, LANES)
    return jnp.broadcast_to(tab, (N_BLOCKS * 2 * SLABS, SUBLANES, LANES))


def _run_stream(x, h_re, h_im, conv_prev, lw, a_tab, bblk, cblk, norm_final_g, *, tt, tm, tf):
    bsz, seqlen, _ = x.shape
    nbg = bsz // SEQ_PER_STEP
    x4 = x.reshape(nbg, SEQ_PER_STEP, seqlen, D_MODEL)
    h0 = _state_to_rows(h_re, h_im)
    c0 = jnp.pad(conv_prev, ((0, 0), (SUBLANES - (CONV_WIDTH - 1), 0), (0, 0)))
    c0 = c0.reshape(nbg, SEQ_PER_STEP, SUBLANES, D_CONV)
    mix, hout, cout = _mixer(x4, h0, c0, lw["norm_mix_g"], lw["w_u"], lw["w_gates"], a_tab, bblk, cblk, lw["d"],
                             lw["w_glu"], lw["b_glu"], lw["w_conv"], lw["b_conv"], tt=tt)
    y = _mlp(x.reshape(bsz * seqlen, D_MODEL), mix.reshape(bsz * seqlen, D_MODEL), lw["w_out"],
             lw["norm_mlp_g"], lw["w_up"], lw["w_down"], norm_final_g, tm=tm, tf=tf)
    new_re, new_im = _rows_to_state(hout)
    new_conv = cout.reshape(bsz, SUBLANES, D_CONV)[:, SUBLANES - (CONV_WIDTH - 1):, :]
    return y.reshape(bsz, seqlen, D_MODEL), new_re, new_im, new_conv


def kernel(x_prompt, x_sample, state_ssm_re, state_ssm_im, cache_conv, norm_mix_g, w_in, ssm_lambda_re, ssm_lambda_im, ssm_log_step, ssm_b_re, ssm_b_im, ssm_c_re, ssm_c_im, ssm_d, w_glu, b_glu, w_conv, b_conv, w_out, norm_mlp_g, w_up, w_down, norm_final_g):
    depth = w_in.shape[0]
    assert depth == 1, "the final norm is fused into the last (only) layer"
    bf16 = jnp.bfloat16
    bp = x_prompt.shape[0]
    a_re, a_im, bblk, cblk = _discretise(ssm_lambda_re[0], ssm_lambda_im[0], ssm_log_step[0],
                                         ssm_b_re[0], ssm_b_im[0], ssm_c_re[0], ssm_c_im[0])
    a_tab = _a_table(a_re, a_im)
    lw = dict(
        norm_mix_g=norm_mix_g[0].reshape(1, D_MODEL), w_u=w_in[0][:, :D_SSM].astype(bf16),
        w_gates=jnp.transpose(w_in[0][:, D_SSM:].reshape(D_MODEL, N_GATE_CHUNKS, GATE_CHUNK), (1, 0, 2)).astype(bf16),
        d=ssm_d[0].reshape(1, D_SSM), w_glu=w_glu[0].astype(bf16), b_glu=b_glu[0].reshape(1, D_SSM),
        w_conv=w_conv[0], b_conv=b_conv[0].reshape(1, D_CONV), w_out=w_out[0].astype(bf16),
        norm_mlp_g=norm_mlp_g[0].reshape(1, D_MODEL), w_up=w_up[0].astype(bf16),
        w_down=w_down[0].astype(bf16))
    gfin = norm_final_g.reshape(1, D_MODEL)
    bblk = bblk.astype(bf16)
    cblk = cblk.astype(bf16)

    zeros_h = jnp.zeros((bp, N_GROUPS, N_STATE), jnp.float32)
    zeros_c = jnp.zeros((bp, CONV_WIDTH - 1, D_CONV), x_prompt.dtype)
    yp, pre, pim, pcv = _run_stream(x_prompt, zeros_h, zeros_h, zeros_c, lw, a_tab, bblk, cblk, gfin,
                                    tt=128, tm=512, tf=1024)
    ys, sre, sim, scv = _run_stream(x_sample, state_ssm_re[0], state_ssm_im[0], cache_conv[0], lw, a_tab,
                                    bblk, cblk, gfin, tt=64, tm=512, tf=1024)
    return (yp, ys, pre[None], pim[None], pcv[None], sre[None], sim[None], scv[None])
```

```python
import functools

import jax
import jax.numpy as jnp
from jax import lax
from jax.experimental import pallas as pl
from jax.experimental.pallas import tpu as pltpu

D_MODEL = 2048
D_SSM = 1024
D_CONV = 1024
N_GROUPS = 64
GROUP_W = 16
N_STATE = 64
D_FF = 8192
D_IN = D_SSM + 3 * D_CONV
CONV_WIDTH = 3
EPS = 1e-6
INV_SQRT2 = 0.7071067811865476

SUBLANES = 8
LANES = 128
SEQ_PER_STEP = 4
GROUPS_PER_BLOCK = 16
N_BLOCKS = N_GROUPS // GROUPS_PER_BLOCK
BLOCK_IN = GROUPS_PER_BLOCK * GROUP_W
BLOCK_STATE = GROUPS_PER_BLOCK * N_STATE
SLABS = BLOCK_STATE // LANES
GATE_CHUNK = 256
PAIR_ROWS = 2 * SEQ_PER_STEP * SUBLANES
VMEM_LIMIT = 60 * 1024 * 1024


def _discretise_kernel(lre_ref, lim_ref, ls_ref, brt_ref, bit_ref, crt_ref, cit_ref,
                       are_ref, aim_ref, naim_ref, bblk_ref, cblk_ref, zr_ref, zi_ref):
    lre = lre_ref[...]
    lim = lim_ref[...]
    delta = jnp.exp(ls_ref[...])
    mag = jnp.exp(lre * delta)
    a_re = mag * jnp.cos(lim * delta)
    a_im = mag * jnp.sin(lim * delta)
    den = lre * lre + lim * lim
    zr_ref[...] = ((a_re - 1.0) * lre + a_im * lim) / den
    zi_ref[...] = (a_im * lre - (a_re - 1.0) * lim) / den
    are_ref[...] = a_re
    aim_ref[...] = a_im
    naim_ref[...] = -a_im
    bblk_ref[...] = jnp.zeros(bblk_ref.shape, jnp.float32)
    cblk_ref[...] = jnp.zeros(cblk_ref.shape, jnp.float32)
    for g in range(N_GROUPS):
        j, gl = divmod(g, GROUPS_PER_BLOCK)
        zr = zr_ref[pl.ds(g, 1), :]
        zi = zi_ref[pl.ds(g, 1), :]
        br = brt_ref[g]
        bi = bit_ref[g]
        rows = slice(gl * GROUP_W, (gl + 1) * GROUP_W)
        cols = slice(gl * N_STATE, (gl + 1) * N_STATE)
        cols_im = slice(BLOCK_STATE + gl * N_STATE, BLOCK_STATE + (gl + 1) * N_STATE)
        bblk_ref[j, rows, cols] = zr * br - zi * bi
        bblk_ref[j, rows, cols_im] = zr * bi + zi * br
        cblk_ref[j, cols, rows] = crt_ref[g]
        cblk_ref[j, cols_im, rows] = -cit_ref[g]


def _discretise(lam_re, lam_im, log_step, b_re, b_im, c_re, c_im):
    f32 = jnp.float32
    brt = jnp.transpose(b_re, (0, 2, 1))
    bit = jnp.transpose(b_im, (0, 2, 1))
    crt = jnp.transpose(c_re, (0, 2, 1))
    cit = jnp.transpose(c_im, (0, 2, 1))
    out_shape = (
        jax.ShapeDtypeStruct((N_GROUPS, N_STATE), f32),
        jax.ShapeDtypeStruct((N_GROUPS, N_STATE), f32),
        jax.ShapeDtypeStruct((N_GROUPS, N_STATE), f32),
        jax.ShapeDtypeStruct((N_BLOCKS, BLOCK_IN, 2 * BLOCK_STATE), f32),
        jax.ShapeDtypeStruct((N_BLOCKS, 2 * BLOCK_STATE, BLOCK_IN), f32),
    )
    return pl.pallas_call(
        _discretise_kernel,
        out_shape=out_shape,
        scratch_shapes=[pltpu.VMEM((N_GROUPS, N_STATE), f32), pltpu.VMEM((N_GROUPS, N_STATE), f32)],
        compiler_params=pltpu.CompilerParams(vmem_limit_bytes=VMEM_LIMIT),
        name="discretise",
    )(lam_re, lam_im, log_step.reshape(N_GROUPS, 1), brt, bit, crt, cit)


def _mixer_kernel(x_ref, h0_ref, c0_ref, g_ref, win_ref, a1_ref, a2_ref, bblk_ref, cblk_ref, d_ref,
                  wglu_ref, bglu_ref, wconv_ref, bconv_ref,
                  mix_ref, hout_ref, cout_ref,
                  bu_ref, perm_ref, vbuf_ref, xn_ref, *, tt):
    f32, bf16 = jnp.float32, jnp.bfloat16
    nseq = SEQ_PER_STEP
    m = nseq * tt
    nth = tt // SUBLANES
    half = nseq * SUBLANES
    ti = pl.program_id(1)

    @pl.when(ti == 0)
    def _():
        hout_ref[...] = h0_ref[...]
        vbuf_ref[:, 0:SUBLANES, :] = c0_ref[...]

    x = x_ref[...].reshape(m, D_MODEL)
    r = lax.rsqrt(jnp.mean(x * x, axis=-1, keepdims=True) + EPS)
    xn_ref[...] = ((x * r) * g_ref[...]).astype(bf16)

    u = jnp.dot(xn_ref[...], win_ref[:, 0:D_SSM], preferred_element_type=f32)

    u4 = u.reshape(nseq, nth, SUBLANES, D_SSM)
    for b in range(nseq):
        perm_ref[:, b] = u4[b]
    up = perm_ref[...].reshape(m, D_SSM).astype(bf16)

    sections = 4
    steps = tt // sections
    per_gate = D_CONV // GATE_CHUNK
    assert per_gate == N_BLOCKS

    def gate_cols(gate, c):
        start = D_SSM + gate * D_CONV + c * GATE_CHUNK
        return slice(start, start + GATE_CHUNK)

    def project_b(j):
        bu = jnp.dot(up[:, j * BLOCK_IN:(j + 1) * BLOCK_IN], bblk_ref[j], preferred_element_type=f32)
        for k in range(SLABS):
            re = bu[:, k * LANES:(k + 1) * LANES]
            im = bu[:, BLOCK_STATE + k * LANES:BLOCK_STATE + (k + 1) * LANES]
            bu_ref[j % 2, k, :, 0:half, :] = re.reshape(nth, half, LANES)
            bu_ref[j % 2, k, :, half:2 * half, :] = im.reshape(nth, half, LANES)

    def advance(j, q, hs, a1, a2):
        buf = j % 2
        for s in range(q * steps, (q + 1) * steps):
            th, tl = divmod(s, SUBLANES)
            rows = pl.ds(tl, 2 * nseq, stride=SUBLANES)
            for k in range(SLABS):
                h = hs[k]
                nh = a1[k] * h + a2[k] * pltpu.roll(h, nseq, axis=0) + bu_ref[buf, k, th, rows, :]
                bu_ref[buf, k, th, rows, :] = nh
                hs[k] = nh
        return hs

    project_b(0)
    for j in range(N_BLOCKS):
        if j + 1 < N_BLOCKS:
            project_b(j + 1)
        a1 = [a1_ref[j * SLABS + k] for k in range(SLABS)]
        a2 = [a2_ref[j * SLABS + k] for k in range(SLABS)]
        hs = [hout_ref[j * SLABS + k] for k in range(SLABS)]

        cols = slice(j * GATE_CHUNK, (j + 1) * GATE_CHUNK)
        gate_b = jnp.dot(xn_ref[...], win_ref[:, gate_cols(0, j)], preferred_element_type=f32)
        hs = advance(j, 0, hs, a1, a2)
        gate_c = jnp.dot(xn_ref[...], win_ref[:, gate_cols(1, j)], preferred_element_type=f32)
        hs = advance(j, 1, hs, a1, a2)
        xv = jnp.dot(xn_ref[...], win_ref[:, gate_cols(2, j)], preferred_element_type=f32)
        hs = advance(j, 2, hs, a1, a2)
        v = gate_c * xv
        vbuf_ref[:, SUBLANES:, cols] = v.reshape(nseq, tt, GATE_CHUNK)
        v1 = vbuf_ref[:, SUBLANES - 1:SUBLANES - 1 + tt, cols].reshape(m, GATE_CHUNK)
        v2 = vbuf_ref[:, SUBLANES - 2:SUBLANES - 2 + tt, cols].reshape(m, GATE_CHUNK)
        yc = (bconv_ref[:, cols] + wconv_ref[0:1, cols] * v2 + wconv_ref[1:2, cols] * v1
              + wconv_ref[2:3, cols] * v)
        mix_ref[:, :, D_SSM + j * GATE_CHUNK:D_SSM + (j + 1) * GATE_CHUNK] = (
            (gate_b * yc).astype(bf16).reshape(nseq, tt, GATE_CHUNK))
        hs = advance(j, 3, hs, a1, a2)

        for k in range(SLABS):
            hout_ref[j * SLABS + k] = hs[k]
        h_all = jnp.concatenate(
            [bu_ref[j % 2, k, :, 0:half, :].reshape(m, LANES) for k in range(SLABS)]
            + [bu_ref[j % 2, k, :, half:2 * half, :].reshape(m, LANES) for k in range(SLABS)],
            axis=1).astype(bf16)
        yj = jnp.dot(h_all, cblk_ref[j], preferred_element_type=f32)
        ucols = slice(j * BLOCK_IN, (j + 1) * BLOCK_IN)
        perm_ref[:, :, :, ucols] = (yj.reshape(nth, nseq, SUBLANES, BLOCK_IN)
                                    + d_ref[:, ucols] * perm_ref[:, :, :, ucols])

    tail = vbuf_ref[:, tt:tt + SUBLANES, :]
    vbuf_ref[:, 0:SUBLANES, :] = tail
    cout_ref[...] = tail

    y = jnp.concatenate([perm_ref[:, b].reshape(tt, D_SSM) for b in range(nseq)], axis=0)
    y = 0.5 * y * (1.0 + lax.erf(y * INV_SQRT2))
    gl = jnp.dot(y.astype(bf16), wglu_ref[...], preferred_element_type=f32) + bglu_ref[...]
    y_ssm = y * jax.nn.sigmoid(gl)
    mix_ref[:, :, 0:D_SSM] = y_ssm.astype(bf16).reshape(nseq, tt, D_SSM)


def _const_spec(shape):
    zeros = (0,) * len(shape)
    return pl.BlockSpec(shape, lambda *_: zeros, pipeline_mode=pl.Buffered(1))


def _mixer(x, h0, c0, norm_g, w_in, a1_tab, a2_tab, bblk, cblk, d, w_glu, b_glu, w_conv, b_conv, *, tt):
    nbg, nseq, seqlen, _ = x.shape
    assert nseq == SEQ_PER_STEP and seqlen % tt == 0 and tt % (4 * SUBLANES) == 0
    nt = seqlen // tt
    m = nseq * tt
    f32 = jnp.float32
    nstate_rows = N_BLOCKS * SLABS
    in_specs = [
        pl.BlockSpec((None, nseq, tt, D_MODEL), lambda bg, ti: (bg, 0, ti, 0)),
        pl.BlockSpec((None, nstate_rows, SUBLANES, LANES), lambda bg, ti: (bg, 0, 0, 0)),
        pl.BlockSpec((None, nseq, SUBLANES, D_CONV), lambda bg, ti: (bg, 0, 0, 0)),
        _const_spec((1, D_MODEL)),
        _const_spec((D_MODEL, D_IN)),
        _const_spec((nstate_rows, SUBLANES, LANES)),
        _const_spec((nstate_rows, SUBLANES, LANES)),
        _const_spec((N_BLOCKS, BLOCK_IN, 2 * BLOCK_STATE)),
        _const_spec((N_BLOCKS, 2 * BLOCK_STATE, BLOCK_IN)),
        _const_spec((1, D_SSM)),
        _const_spec((D_SSM, D_SSM)),
        _const_spec((1, D_SSM)),
        _const_spec((CONV_WIDTH, D_CONV)),
        _const_spec((1, D_CONV)),
    ]
    out_specs = [
        pl.BlockSpec((None, nseq, tt, D_MODEL), lambda bg, ti: (bg, 0, ti, 0)),
        pl.BlockSpec((None, nstate_rows, SUBLANES, LANES), lambda bg, ti: (bg, 0, 0, 0)),
        pl.BlockSpec((None, nseq, SUBLANES, D_CONV), lambda bg, ti: (bg, 0, 0, 0)),
    ]
    out_shape = (
        jax.ShapeDtypeStruct((nbg, nseq, seqlen, D_MODEL), jnp.bfloat16),
        jax.ShapeDtypeStruct((nbg, nstate_rows, SUBLANES, LANES), f32),
        jax.ShapeDtypeStruct((nbg, nseq, SUBLANES, D_CONV), f32),
    )
    scratch = [
        pltpu.VMEM((2, SLABS, tt // SUBLANES, PAIR_ROWS, LANES), f32),
        pltpu.VMEM((tt // SUBLANES, nseq, SUBLANES, D_SSM), f32),
        pltpu.VMEM((nseq, tt + SUBLANES, D_CONV), f32),
        pltpu.VMEM((m, D_MODEL), jnp.bfloat16),
    ]
    return pl.pallas_call(
        functools.partial(_mixer_kernel, tt=tt),
        grid=(nbg, nt),
        in_specs=in_specs,
        out_specs=out_specs,
        out_shape=out_shape,
        scratch_shapes=scratch,
        compiler_params=pltpu.CompilerParams(
            dimension_semantics=("arbitrary", "arbitrary"), vmem_limit_bytes=VMEM_LIMIT),
        name="mixer",
    )(x, h0, c0, norm_g, w_in, a1_tab, a2_tab, bblk, cblk, d, w_glu, b_glu, w_conv, b_conv)


def _mlp_kernel(x_ref, mix_ref, wout_ref, g_ref, wup_ref, wdown_ref, gfin_ref, o_ref, xn_ref):
    f32, bf16 = jnp.float32, jnp.bfloat16
    f = pl.program_id(1)
    last = pl.num_programs(1) - 1
    tm = o_ref.shape[0]
    halves = (slice(0, tm // 2), slice(tm // 2, tm))

    def ffn(xn, acc):
        hm = jnp.dot(xn, wup_ref[...], preferred_element_type=f32)
        act = jnp.square(jnp.maximum(hm, 0.0)).astype(bf16)
        return acc + jnp.dot(act, wdown_ref[...], preferred_element_type=f32)

    @pl.when(f == 0)
    def _():
        for rows in halves:
            x1 = x_ref[rows, :] + jnp.dot(mix_ref[rows, :], wout_ref[...], preferred_element_type=f32)
            r = lax.rsqrt(jnp.mean(x1 * x1, axis=-1, keepdims=True) + EPS)
            xn = ((x1 * r) * g_ref[...]).astype(bf16)
            xn_ref[rows, :] = xn
            o_ref[rows, :] = ffn(xn, x1)

    @pl.when(jnp.logical_and(f > 0, f < last))
    def _():
        o_ref[...] = ffn(xn_ref[...], o_ref[...])

    @pl.when(f == last)
    def _():
        for rows in halves:
            x2 = ffn(xn_ref[rows, :], o_ref[rows, :])
            r = lax.rsqrt(jnp.mean(x2 * x2, axis=-1, keepdims=True) + EPS)
            o_ref[rows, :] = (x2 * r) * gfin_ref[...]


def _mlp(x, mix, w_out, norm_g, w_up, w_down, norm_final_g, *, tm, tf):
    tokens = x.shape[0]
    assert tokens % tm == 0 and D_FF % tf == 0 and D_FF // tf >= 2
    in_specs = [
        pl.BlockSpec((tm, D_MODEL), lambda i, f: (i, 0)),
        pl.BlockSpec((tm, D_MODEL), lambda i, f: (i, 0)),
        _const_spec((D_MODEL, D_MODEL)),
        _const_spec((1, D_MODEL)),
        pl.BlockSpec((D_MODEL, tf), lambda i, f: (0, f)),
        pl.BlockSpec((tf, D_MODEL), lambda i, f: (f, 0)),
        _const_spec((1, D_MODEL)),
    ]
    return pl.pallas_call(
        _mlp_kernel,
        grid=(tokens // tm, D_FF // tf),
        in_specs=in_specs,
        out_specs=pl.BlockSpec((tm, D_MODEL), lambda i, f: (i, 0)),
        out_shape=jax.ShapeDtypeStruct((tokens, D_MODEL), jnp.float32),
        scratch_shapes=[pltpu.VMEM((tm, D_MODEL), jnp.bfloat16)],
        compiler_params=pltpu.CompilerParams(
            dimension_semantics=("arbitrary", "arbitrary"), vmem_limit_bytes=VMEM_LIMIT),
        name="mlp",
    )(x, mix, w_out, norm_g, w_up, w_down, norm_final_g)


def _state_to_rows(h_re, h_im):
    def one(h):
        nb = h.shape[0]
        h = h.reshape(nb // SEQ_PER_STEP, SEQ_PER_STEP, N_BLOCKS, SLABS, LANES)
        return jnp.transpose(h, (0, 2, 3, 1, 4))
    both = jnp.concatenate([one(h_re), one(h_im)], axis=3)
    return both.reshape(both.shape[0], N_BLOCKS * SLABS, 2 * SEQ_PER_STEP, LANES)


def _rows_to_state(rows):
    nbg = rows.shape[0]
    r = rows.reshape(nbg, N_BLOCKS, SLABS, 2, SEQ_PER_STEP, LANES)
    r = jnp.transpose(r, (3, 0, 4, 1, 2, 5))
    r = r.reshape(2, nbg * SEQ_PER_STEP, N_GROUPS, N_STATE)
    return r[0], r[1]


def _a_tables(a_re, a_im, neg_a_im):
    def rows(t, n):
        return jnp.broadcast_to(t.reshape(N_BLOCKS * SLABS, 1, LANES), (N_BLOCKS * SLABS, n, LANES))
    a1 = rows(a_re, 2 * SEQ_PER_STEP)
    a2 = jnp.concatenate([rows(neg_a_im, SEQ_PER_STEP), rows(a_im, SEQ_PER_STEP)], axis=1)
    return a1, a2


def _run_stream(x, h_re, h_im, conv_prev, lw, tabs, norm_final_g, *, tt, tm, tf):
    bsz, seqlen, _ = x.shape
    nbg = bsz // SEQ_PER_STEP
    x4 = x.reshape(nbg, SEQ_PER_STEP, seqlen, D_MODEL)
    h0 = _state_to_rows(h_re, h_im)
    c0 = jnp.pad(conv_prev, ((0, 0), (SUBLANES - (CONV_WIDTH - 1), 0), (0, 0)))
    c0 = c0.reshape(nbg, SEQ_PER_STEP, SUBLANES, D_CONV)
    mix, hout, cout = _mixer(x4, h0, c0, lw["norm_mix_g"], lw["w_in"], *tabs, lw["d"],
                             lw["w_glu"], lw["b_glu"], lw["w_conv"], lw["b_conv"], tt=tt)
    y = _mlp(x.reshape(bsz * seqlen, D_MODEL), mix.reshape(bsz * seqlen, D_MODEL), lw["w_out"],
             lw["norm_mlp_g"], lw["w_up"], lw["w_down"], norm_final_g, tm=tm, tf=tf)
    new_re, new_im = _rows_to_state(hout)
    new_conv = cout.reshape(bsz, SUBLANES, D_CONV)[:, SUBLANES - (CONV_WIDTH - 1):, :]
    return y.reshape(bsz, seqlen, D_MODEL), new_re, new_im, new_conv


def kernel(x_prompt, x_sample, state_ssm_re, state_ssm_im, cache_conv, norm_mix_g, w_in, ssm_lambda_re, ssm_lambda_im, ssm_log_step, ssm_b_re, ssm_b_im, ssm_c_re, ssm_c_im, ssm_d, w_glu, b_glu, w_conv, b_conv, w_out, norm_mlp_g, w_up, w_down, norm_final_g):
    depth = w_in.shape[0]
    assert depth == 1, "the final norm is fused into the last (only) layer"
    bf16 = jnp.bfloat16
    bp = x_prompt.shape[0]
    a_re, a_im, neg_a_im, bblk, cblk = _discretise(
        ssm_lambda_re[0], ssm_lambda_im[0], ssm_log_step[0], ssm_b_re[0], ssm_b_im[0], ssm_c_re[0], ssm_c_im[0])
    tabs = _a_tables(a_re, a_im, neg_a_im) + (bblk.astype(bf16), cblk.astype(bf16))
    lw = dict(
        norm_mix_g=norm_mix_g[0].reshape(1, D_MODEL), w_in=w_in[0].astype(bf16),
        d=ssm_d[0].reshape(1, D_SSM), w_glu=w_glu[0].astype(bf16), b_glu=b_glu[0].reshape(1, D_SSM),
        w_conv=w_conv[0], b_conv=b_conv[0].reshape(1, D_CONV), w_out=w_out[0].astype(bf16),
        norm_mlp_g=norm_mlp_g[0].reshape(1, D_MODEL), w_up=w_up[0].astype(bf16),
        w_down=w_down[0].astype(bf16))
    gfin = norm_final_g.reshape(1, D_MODEL)

    zeros_h = jnp.zeros((bp, N_GROUPS, N_STATE), jnp.float32)
    zeros_c = jnp.zeros((bp, CONV_WIDTH - 1, D_CONV), x_prompt.dtype)
    yp, pre, pim, pcv = _run_stream(x_prompt, zeros_h, zeros_h, zeros_c, lw, tabs, gfin,
                                    tt=128, tm=512, tf=1024)
    ys, sre, sim, scv = _run_stream(x_sample, state_ssm_re[0], state_ssm_im[0], cache_conv[0], lw, tabs, gfin,
                                    tt=64, tm=512, tf=1024)
    return (yp, ys, pre[None], pim[None], pcv[None], sre[None], sim[None], scv[None])
```

```python
import functools

import jax
import jax.numpy as jnp
from jax import lax
from jax.experimental import pallas as pl
from jax.experimental.pallas import tpu as pltpu

D_MODEL = 2048
D_SSM = 1024
D_CONV = 1024
N_GROUPS = 64
GROUP_W = 16
N_STATE = 64
D_FF = 8192
D_IN = D_SSM + 3 * D_CONV
CONV_WIDTH = 3
EPS = 1e-6
INV_SQRT2 = 0.7071067811865476

SUBLANES = 8
LANES = 128
SEQ_PER_STEP = 4
GROUPS_PER_BLOCK = 16
N_BLOCKS = N_GROUPS // GROUPS_PER_BLOCK
BLOCK_IN = GROUPS_PER_BLOCK * GROUP_W
BLOCK_STATE = GROUPS_PER_BLOCK * N_STATE
SLABS = BLOCK_STATE // LANES
GATE_CHUNK = 256
PAIR_ROWS = 2 * SEQ_PER_STEP * SUBLANES
VMEM_LIMIT = 60 * 1024 * 1024


def _discretise_kernel(lre_ref, lim_ref, ls_ref, brt_ref, bit_ref, crt_ref, cit_ref,
                       are_ref, aim_ref, naim_ref, bblk_ref, cblk_ref, zr_ref, zi_ref):
    lre = lre_ref[...]
    lim = lim_ref[...]
    delta = jnp.exp(ls_ref[...])
    mag = jnp.exp(lre * delta)
    a_re = mag * jnp.cos(lim * delta)
    a_im = mag * jnp.sin(lim * delta)
    den = lre * lre + lim * lim
    zr_ref[...] = ((a_re - 1.0) * lre + a_im * lim) / den
    zi_ref[...] = (a_im * lre - (a_re - 1.0) * lim) / den
    are_ref[...] = a_re
    aim_ref[...] = a_im
    naim_ref[...] = -a_im
    bblk_ref[...] = jnp.zeros(bblk_ref.shape, jnp.float32)
    cblk_ref[...] = jnp.zeros(cblk_ref.shape, jnp.float32)
    for g in range(N_GROUPS):
        j, gl = divmod(g, GROUPS_PER_BLOCK)
        zr = zr_ref[pl.ds(g, 1), :]
        zi = zi_ref[pl.ds(g, 1), :]
        br = brt_ref[g]
        bi = bit_ref[g]
        rows = slice(gl * GROUP_W, (gl + 1) * GROUP_W)
        cols = slice(gl * N_STATE, (gl + 1) * N_STATE)
        cols_im = slice(BLOCK_STATE + gl * N_STATE, BLOCK_STATE + (gl + 1) * N_STATE)
        bblk_ref[j, rows, cols] = zr * br - zi * bi
        bblk_ref[j, rows, cols_im] = zr * bi + zi * br
        cblk_ref[j, cols, rows] = crt_ref[g]
        cblk_ref[j, cols_im, rows] = -cit_ref[g]


def _discretise(lam_re, lam_im, log_step, b_re, b_im, c_re, c_im):
    f32 = jnp.float32
    brt = jnp.transpose(b_re, (0, 2, 1))
    bit = jnp.transpose(b_im, (0, 2, 1))
    crt = jnp.transpose(c_re, (0, 2, 1))
    cit = jnp.transpose(c_im, (0, 2, 1))
    out_shape = (
        jax.ShapeDtypeStruct((N_GROUPS, N_STATE), f32),
        jax.ShapeDtypeStruct((N_GROUPS, N_STATE), f32),
        jax.ShapeDtypeStruct((N_GROUPS, N_STATE), f32),
        jax.ShapeDtypeStruct((N_BLOCKS, BLOCK_IN, 2 * BLOCK_STATE), f32),
        jax.ShapeDtypeStruct((N_BLOCKS, 2 * BLOCK_STATE, BLOCK_IN), f32),
    )
    return pl.pallas_call(
        _discretise_kernel,
        out_shape=out_shape,
        scratch_shapes=[pltpu.VMEM((N_GROUPS, N_STATE), f32), pltpu.VMEM((N_GROUPS, N_STATE), f32)],
        compiler_params=pltpu.CompilerParams(vmem_limit_bytes=VMEM_LIMIT),
        name="discretise",
    )(lam_re, lam_im, log_step.reshape(N_GROUPS, 1), brt, bit, crt, cit)


def _mixer_kernel(x_ref, h0_ref, c0_ref, g_ref, win_ref, a1_ref, a2_ref, bblk_ref, cblk_ref, d_ref,
                  wglu_ref, bglu_ref, wconv_ref, bconv_ref,
                  mix_ref, hout_ref, cout_ref,
                  bu_ref, perm_ref, vbuf_ref, xn_ref, *, tt):
    f32, bf16 = jnp.float32, jnp.bfloat16
    nseq = SEQ_PER_STEP
    m = nseq * tt
    nth = tt // SUBLANES
    half = nseq * SUBLANES
    ti = pl.program_id(1)

    @pl.when(ti == 0)
    def _():
        hout_ref[...] = h0_ref[...]
        vbuf_ref[:, 0:SUBLANES, :] = c0_ref[...]

    x = x_ref[...].reshape(m, D_MODEL)
    r = lax.rsqrt(jnp.mean(x * x, axis=-1, keepdims=True) + EPS)
    xn_ref[...] = ((x * r) * g_ref[...]).astype(bf16)

    u = jnp.dot(xn_ref[...], win_ref[:, 0:D_SSM], preferred_element_type=f32)

    u4 = u.reshape(nseq, nth, SUBLANES, D_SSM)
    for b in range(nseq):
        perm_ref[:, b] = u4[b]
    up = perm_ref[...].reshape(m, D_SSM).astype(bf16)

    sections = 4
    steps = tt // sections
    per_gate = D_CONV // GATE_CHUNK
    assert per_gate == N_BLOCKS

    def gate_cols(gate, c):
        start = D_SSM + gate * D_CONV + c * GATE_CHUNK
        return slice(start, start + GATE_CHUNK)

    def project_b(j):
        bu = jnp.dot(up[:, j * BLOCK_IN:(j + 1) * BLOCK_IN], bblk_ref[j], preferred_element_type=f32)
        for k in range(SLABS):
            re = bu[:, k * LANES:(k + 1) * LANES]
            im = bu[:, BLOCK_STATE + k * LANES:BLOCK_STATE + (k + 1) * LANES]
            bu_ref[j % 2, k, :, 0:half, :] = re.reshape(nth, half, LANES)
            bu_ref[j % 2, k, :, half:2 * half, :] = im.reshape(nth, half, LANES)

    def advance(j, q, hs, a1, a2):
        buf = j % 2
        for s in range(q * steps, (q + 1) * steps):
            th, tl = divmod(s, SUBLANES)
            rows = pl.ds(tl, 2 * nseq, stride=SUBLANES)
            for k in range(SLABS):
                h = hs[k]
                nh = a1[k] * h + a2[k] * pltpu.roll(h, nseq, axis=0) + bu_ref[buf, k, th, rows, :]
                bu_ref[buf, k, th, rows, :] = nh
                hs[k] = nh
        return hs

    project_b(0)
    for j in range(N_BLOCKS):
        if j + 1 < N_BLOCKS:
            project_b(j + 1)
        a1 = [a1_ref[j * SLABS + k] for k in range(SLABS)]
        a2 = [a2_ref[j * SLABS + k] for k in range(SLABS)]
        hs = [hout_ref[j * SLABS + k] for k in range(SLABS)]

        cols = slice(j * GATE_CHUNK, (j + 1) * GATE_CHUNK)
        gate_b = jnp.dot(xn_ref[...], win_ref[:, gate_cols(0, j)], preferred_element_type=f32)
        hs = advance(j, 0, hs, a1, a2)
        gate_c = jnp.dot(xn_ref[...], win_ref[:, gate_cols(1, j)], preferred_element_type=f32)
        hs = advance(j, 1, hs, a1, a2)
        xv = jnp.dot(xn_ref[...], win_ref[:, gate_cols(2, j)], preferred_element_type=f32)
        hs = advance(j, 2, hs, a1, a2)
        v = gate_c * xv
        vbuf_ref[:, SUBLANES:, cols] = v.reshape(nseq, tt, GATE_CHUNK)
        v1 = vbuf_ref[:, SUBLANES - 1:SUBLANES - 1 + tt, cols].reshape(m, GATE_CHUNK)
        v2 = vbuf_ref[:, SUBLANES - 2:SUBLANES - 2 + tt, cols].reshape(m, GATE_CHUNK)
        yc = (bconv_ref[:, cols] + wconv_ref[0:1, cols] * v2 + wconv_ref[1:2, cols] * v1
              + wconv_ref[2:3, cols] * v)
        mix_ref[:, :, D_SSM + j * GATE_CHUNK:D_SSM + (j + 1) * GATE_CHUNK] = (
            (gate_b * yc).astype(bf16).reshape(nseq, tt, GATE_CHUNK))
        hs = advance(j, 3, hs, a1, a2)

        for k in range(SLABS):
            hout_ref[j * SLABS + k] = hs[k]
        h_all = jnp.concatenate(
            [bu_ref[j % 2, k, :, 0:half, :].reshape(m, LANES) for k in range(SLABS)]
            + [bu_ref[j % 2, k, :, half:2 * half, :].reshape(m, LANES) for k in range(SLABS)],
            axis=1).astype(bf16)
        yj = jnp.dot(h_all, cblk_ref[j], preferred_element_type=f32)
        ucols = slice(j * BLOCK_IN, (j + 1) * BLOCK_IN)
        perm_ref[:, :, :, ucols] = (yj.reshape(nth, nseq, SUBLANES, BLOCK_IN)
                                    + d_ref[:, ucols] * perm_ref[:, :, :, ucols])

    tail = vbuf_ref[:, tt:tt + SUBLANES, :]
    vbuf_ref[:, 0:SUBLANES, :] = tail
    cout_ref[...] = tail

    y = jnp.concatenate([perm_ref[:, b].reshape(tt, D_SSM) for b in range(nseq)], axis=0)
    y = 0.5 * y * (1.0 + lax.erf(y * INV_SQRT2))
    gl = jnp.dot(y.astype(bf16), wglu_ref[...], preferred_element_type=f32) + bglu_ref[...]
    y_ssm = y * jax.nn.sigmoid(gl)
    mix_ref[:, :, 0:D_SSM] = y_ssm.astype(bf16).reshape(nseq, tt, D_SSM)


def _const_spec(shape):
    zeros = (0,) * len(shape)
    return pl.BlockSpec(shape, lambda *_: zeros, pipeline_mode=pl.Buffered(1))


def _mixer(x, h0, c0, norm_g, w_in, a1_tab, a2_tab, bblk, cblk, d, w_glu, b_glu, w_conv, b_conv, *, tt):
    nbg, nseq, seqlen, _ = x.shape
    assert nseq == SEQ_PER_STEP and seqlen % tt == 0 and tt % (4 * SUBLANES) == 0
    nt = seqlen // tt
    m = nseq * tt
    f32 = jnp.float32
    nstate_rows = N_BLOCKS * SLABS
    in_specs = [
        pl.BlockSpec((None, nseq, tt, D_MODEL), lambda bg, ti: (bg, 0, ti, 0)),
        pl.BlockSpec((None, nstate_rows, SUBLANES, LANES), lambda bg, ti: (bg, 0, 0, 0)),
        pl.BlockSpec((None, nseq, SUBLANES, D_CONV), lambda bg, ti: (bg, 0, 0, 0)),
        _const_spec((1, D_MODEL)),
        _const_spec((D_MODEL, D_IN)),
        _const_spec((nstate_rows, SUBLANES, LANES)),
        _const_spec((nstate_rows, SUBLANES, LANES)),
        _const_spec((N_BLOCKS, BLOCK_IN, 2 * BLOCK_STATE)),
        _const_spec((N_BLOCKS, 2 * BLOCK_STATE, BLOCK_IN)),
        _const_spec((1, D_SSM)),
        _const_spec((D_SSM, D_SSM)),
        _const_spec((1, D_SSM)),
        _const_spec((CONV_WIDTH, D_CONV)),
        _const_spec((1, D_CONV)),
    ]
    out_specs = [
        pl.BlockSpec((None, nseq, tt, D_MODEL), lambda bg, ti: (bg, 0, ti, 0)),
        pl.BlockSpec((None, nstate_rows, SUBLANES, LANES), lambda bg, ti: (bg, 0, 0, 0)),
        pl.BlockSpec((None, nseq, SUBLANES, D_CONV), lambda bg, ti: (bg, 0, 0, 0)),
    ]
    out_shape = (
        jax.ShapeDtypeStruct((nbg, nseq, seqlen, D_MODEL), jnp.bfloat16),
        jax.ShapeDtypeStruct((nbg, nstate_rows, SUBLANES, LANES), f32),
        jax.ShapeDtypeStruct((nbg, nseq, SUBLANES, D_CONV), f32),
    )
    scratch = [
        pltpu.VMEM((2, SLABS, tt // SUBLANES, PAIR_ROWS, LANES), f32),
        pltpu.VMEM((tt // SUBLANES, nseq, SUBLANES, D_SSM), f32),
        pltpu.VMEM((nseq, tt + SUBLANES, D_CONV), f32),
        pltpu.VMEM((m, D_MODEL), jnp.bfloat16),
    ]
    return pl.pallas_call(
        functools.partial(_mixer_kernel, tt=tt),
        grid=(nbg, nt),
        in_specs=in_specs,
        out_specs=out_specs,
        out_shape=out_shape,
        scratch_shapes=scratch,
        compiler_params=pltpu.CompilerParams(
            dimension_semantics=("arbitrary", "arbitrary"), vmem_limit_bytes=VMEM_LIMIT),
        name="mixer",
    )(x, h0, c0, norm_g, w_in, a1_tab, a2_tab, bblk, cblk, d, w_glu, b_glu, w_conv, b_conv)


def _mlp_kernel(x_ref, mix_ref, wout_ref, g_ref, wup_ref, wdown_ref, gfin_ref, o_ref, xn_ref):
    f32, bf16 = jnp.float32, jnp.bfloat16
    f = pl.program_id(1)

    @pl.when(f == 0)
    def _():
        x1 = x_ref[...] + jnp.dot(mix_ref[...], wout_ref[...], preferred_element_type=f32)
        o_ref[...] = x1
        r = lax.rsqrt(jnp.mean(x1 * x1, axis=-1, keepdims=True) + EPS)
        xn_ref[...] = ((x1 * r) * g_ref[...]).astype(bf16)

    hm = jnp.dot(xn_ref[...], wup_ref[...], preferred_element_type=f32)
    act = jnp.square(jnp.maximum(hm, 0.0)).astype(bf16)
    o_ref[...] += jnp.dot(act, wdown_ref[...], preferred_element_type=f32)

    @pl.when(f == pl.num_programs(1) - 1)
    def _():
        x2 = o_ref[...]
        r = lax.rsqrt(jnp.mean(x2 * x2, axis=-1, keepdims=True) + EPS)
        o_ref[...] = (x2 * r) * gfin_ref[...]


def _mlp(x, mix, w_out, norm_g, w_up, w_down, norm_final_g, *, tm, tf):
    tokens = x.shape[0]
    assert tokens % tm == 0 and D_FF % tf == 0
    in_specs = [
        pl.BlockSpec((tm, D_MODEL), lambda i, f: (i, 0)),
        pl.BlockSpec((tm, D_MODEL), lambda i, f: (i, 0)),
        _const_spec((D_MODEL, D_MODEL)),
        _const_spec((1, D_MODEL)),
        pl.BlockSpec((D_MODEL, tf), lambda i, f: (0, f)),
        pl.BlockSpec((tf, D_MODEL), lambda i, f: (f, 0)),
        _const_spec((1, D_MODEL)),
    ]
    return pl.pallas_call(
        _mlp_kernel,
        grid=(tokens // tm, D_FF // tf),
        in_specs=in_specs,
        out_specs=pl.BlockSpec((tm, D_MODEL), lambda i, f: (i, 0)),
        out_shape=jax.ShapeDtypeStruct((tokens, D_MODEL), jnp.float32),
        scratch_shapes=[pltpu.VMEM((tm, D_MODEL), jnp.bfloat16)],
        compiler_params=pltpu.CompilerParams(
            dimension_semantics=("arbitrary", "arbitrary"), vmem_limit_bytes=VMEM_LIMIT),
        name="mlp",
    )(x, mix, w_out, norm_g, w_up, w_down, norm_final_g)


def _mlp_cast_kernel(x_ref, mix_ref, wout_ref, g_ref, wup_ref, wdown_ref, gfin_ref,
                     o_ref, wup_bf_ref, wdown_bf_ref, xn_ref):
    f32, bf16 = jnp.float32, jnp.bfloat16
    f = pl.program_id(0)
    wup = wup_ref[...].astype(bf16)
    wdown = wdown_ref[...].astype(bf16)
    wup_bf_ref[...] = wup
    wdown_bf_ref[...] = wdown

    @pl.when(f == 0)
    def _():
        x1 = x_ref[...] + jnp.dot(mix_ref[...], wout_ref[...], preferred_element_type=f32)
        o_ref[...] = x1
        r = lax.rsqrt(jnp.mean(x1 * x1, axis=-1, keepdims=True) + EPS)
        xn_ref[...] = ((x1 * r) * g_ref[...]).astype(bf16)

    hm = jnp.dot(xn_ref[...], wup, preferred_element_type=f32)
    act = jnp.square(jnp.maximum(hm, 0.0)).astype(bf16)
    o_ref[...] += jnp.dot(act, wdown, preferred_element_type=f32)

    @pl.when(f == pl.num_programs(0) - 1)
    def _():
        x2 = o_ref[...]
        r = lax.rsqrt(jnp.mean(x2 * x2, axis=-1, keepdims=True) + EPS)
        o_ref[...] = (x2 * r) * gfin_ref[...]


def _mlp_cast(x, mix, w_out, norm_g, w_up_f32, w_down_f32, norm_final_g, *, tf):
    tokens = x.shape[0]
    assert D_FF % tf == 0
    bf16 = jnp.bfloat16
    in_specs = [
        _const_spec((tokens, D_MODEL)),
        _const_spec((tokens, D_MODEL)),
        _const_spec((D_MODEL, D_MODEL)),
        _const_spec((1, D_MODEL)),
        pl.BlockSpec((D_MODEL, tf), lambda f: (0, f)),
        pl.BlockSpec((tf, D_MODEL), lambda f: (f, 0)),
        _const_spec((1, D_MODEL)),
    ]
    out_specs = [
        pl.BlockSpec((tokens, D_MODEL), lambda f: (0, 0)),
        pl.BlockSpec((D_MODEL, tf), lambda f: (0, f)),
        pl.BlockSpec((tf, D_MODEL), lambda f: (f, 0)),
    ]
    out_shape = (
        jax.ShapeDtypeStruct((tokens, D_MODEL), jnp.float32),
        jax.ShapeDtypeStruct((D_MODEL, D_FF), bf16),
        jax.ShapeDtypeStruct((D_FF, D_MODEL), bf16),
    )
    return pl.pallas_call(
        _mlp_cast_kernel,
        grid=(D_FF // tf,),
        in_specs=in_specs,
        out_specs=out_specs,
        out_shape=out_shape,
        scratch_shapes=[pltpu.VMEM((tokens, D_MODEL), bf16)],
        compiler_params=pltpu.CompilerParams(
            dimension_semantics=("arbitrary",), vmem_limit_bytes=VMEM_LIMIT),
        name="mlp_cast",
    )(x, mix, w_out, norm_g, w_up_f32, w_down_f32, norm_final_g)


def _state_to_rows(h_re, h_im):
    def one(h):
        nb = h.shape[0]
        h = h.reshape(nb // SEQ_PER_STEP, SEQ_PER_STEP, N_BLOCKS, SLABS, LANES)
        return jnp.transpose(h, (0, 2, 3, 1, 4))
    both = jnp.concatenate([one(h_re), one(h_im)], axis=3)
    return both.reshape(both.shape[0], N_BLOCKS * SLABS, 2 * SEQ_PER_STEP, LANES)


def _rows_to_state(rows):
    nbg = rows.shape[0]
    r = rows.reshape(nbg, N_BLOCKS, SLABS, 2, SEQ_PER_STEP, LANES)
    r = jnp.transpose(r, (3, 0, 4, 1, 2, 5))
    r = r.reshape(2, nbg * SEQ_PER_STEP, N_GROUPS, N_STATE)
    return r[0], r[1]


def _a_tables(a_re, a_im, neg_a_im):
    def rows(t, n):
        return jnp.broadcast_to(t.reshape(N_BLOCKS * SLABS, 1, LANES), (N_BLOCKS * SLABS, n, LANES))
    a1 = rows(a_re, 2 * SEQ_PER_STEP)
    a2 = jnp.concatenate([rows(neg_a_im, SEQ_PER_STEP), rows(a_im, SEQ_PER_STEP)], axis=1)
    return a1, a2


def _run_mixer(x, h_re, h_im, conv_prev, lw, tabs, *, tt):
    bsz, seqlen, _ = x.shape
    nbg = bsz // SEQ_PER_STEP
    x4 = x.reshape(nbg, SEQ_PER_STEP, seqlen, D_MODEL)
    h0 = _state_to_rows(h_re, h_im)
    c0 = jnp.pad(conv_prev, ((0, 0), (SUBLANES - (CONV_WIDTH - 1), 0), (0, 0)))
    c0 = c0.reshape(nbg, SEQ_PER_STEP, SUBLANES, D_CONV)
    mix, hout, cout = _mixer(x4, h0, c0, lw["norm_mix_g"], lw["w_in"], *tabs, lw["d"],
                             lw["w_glu"], lw["b_glu"], lw["w_conv"], lw["b_conv"], tt=tt)
    new_re, new_im = _rows_to_state(hout)
    new_conv = cout.reshape(bsz, SUBLANES, D_CONV)[:, SUBLANES - (CONV_WIDTH - 1):, :]
    return mix.reshape(bsz * seqlen, D_MODEL), new_re, new_im, new_conv


def kernel(x_prompt, x_sample, state_ssm_re, state_ssm_im, cache_conv, norm_mix_g, w_in, ssm_lambda_re, ssm_lambda_im, ssm_log_step, ssm_b_re, ssm_b_im, ssm_c_re, ssm_c_im, ssm_d, w_glu, b_glu, w_conv, b_conv, w_out, norm_mlp_g, w_up, w_down, norm_final_g):
    depth = w_in.shape[0]
    assert depth == 1, "the final norm is fused into the last (only) layer"
    bf16 = jnp.bfloat16
    bp = x_prompt.shape[0]
    a_re, a_im, neg_a_im, bblk, cblk = _discretise(
        ssm_lambda_re[0], ssm_lambda_im[0], ssm_log_step[0], ssm_b_re[0], ssm_b_im[0], ssm_c_re[0], ssm_c_im[0])
    tabs = _a_tables(a_re, a_im, neg_a_im) + (bblk.astype(bf16), cblk.astype(bf16))
    lw = dict(
        norm_mix_g=norm_mix_g[0].reshape(1, D_MODEL), w_in=w_in[0].astype(bf16),
        d=ssm_d[0].reshape(1, D_SSM), w_glu=w_glu[0].astype(bf16), b_glu=b_glu[0].reshape(1, D_SSM),
        w_conv=w_conv[0], b_conv=b_conv[0].reshape(1, D_CONV), w_out=w_out[0].astype(bf16),
        norm_mlp_g=norm_mlp_g[0].reshape(1, D_MODEL))
    gfin = norm_final_g.reshape(1, D_MODEL)

    zeros_h = jnp.zeros((bp, N_GROUPS, N_STATE), jnp.float32)
    zeros_c = jnp.zeros((bp, CONV_WIDTH - 1, D_CONV), x_prompt.dtype)
    mix_s, sre, sim, scv = _run_mixer(x_sample, state_ssm_re[0], state_ssm_im[0], cache_conv[0], lw, tabs, tt=64)
    mix_p, pre, pim, pcv = _run_mixer(x_prompt, zeros_h, zeros_h, zeros_c, lw, tabs, tt=128)
    ys, w_up_bf, w_down_bf = _mlp_cast(x_sample.reshape(-1, D_MODEL), mix_s, lw["w_out"], lw["norm_mlp_g"],
                                       w_up.reshape(D_MODEL, D_FF), w_down.reshape(D_FF, D_MODEL), gfin, tf=512)
    yp = _mlp(x_prompt.reshape(-1, D_MODEL), mix_p, lw["w_out"], lw["norm_mlp_g"], w_up_bf, w_down_bf, gfin,
              tm=512, tf=1024)
    return (yp.reshape(x_prompt.shape), ys.reshape(x_sample.shape), pre[None], pim[None], pcv[None],
            sre[None], sim[None], scv[None])
```

```python
import functools

import jax
import jax.numpy as jnp
from jax import lax
from jax.experimental import pallas as pl
from jax.experimental.pallas import tpu as pltpu

D_MODEL = 2048
D_SSM = 1024
D_CONV = 1024
N_GROUPS = 64
GROUP_W = 16
N_STATE = 64
D_FF = 8192
D_IN = D_SSM + 3 * D_CONV
CONV_WIDTH = 3
EPS = 1e-6
INV_SQRT2 = 0.7071067811865476

SUBLANES = 8
LANES = 128
SEQ_PER_STEP = 4
GROUPS_PER_BLOCK = 16
N_BLOCKS = N_GROUPS // GROUPS_PER_BLOCK
BLOCK_IN = GROUPS_PER_BLOCK * GROUP_W
BLOCK_STATE = GROUPS_PER_BLOCK * N_STATE
SLABS = BLOCK_STATE // LANES
GATE_CHUNK = 256
QUAD = 4
QUAD_ROWS = 2 * SEQ_PER_STEP * QUAD
VMEM_LIMIT = 60 * 1024 * 1024


def _discretise_kernel(lre_ref, lim_ref, ls_ref, brt_ref, bit_ref, crt_ref, cit_ref,
                       are_ref, aim_ref, naim_ref, bblk_ref, cblk_ref, zr_ref, zi_ref):
    lre = lre_ref[...]
    lim = lim_ref[...]
    delta = jnp.exp(ls_ref[...])
    mag = jnp.exp(lre * delta)
    a_re = mag * jnp.cos(lim * delta)
    a_im = mag * jnp.sin(lim * delta)
    den = lre * lre + lim * lim
    zr_ref[...] = ((a_re - 1.0) * lre + a_im * lim) / den
    zi_ref[...] = (a_im * lre - (a_re - 1.0) * lim) / den
    are_ref[...] = a_re
    aim_ref[...] = a_im
    naim_ref[...] = -a_im
    bblk_ref[...] = jnp.zeros(bblk_ref.shape, jnp.float32)
    cblk_ref[...] = jnp.zeros(cblk_ref.shape, jnp.float32)
    for g in range(N_GROUPS):
        j, gl = divmod(g, GROUPS_PER_BLOCK)
        zr = zr_ref[pl.ds(g, 1), :]
        zi = zi_ref[pl.ds(g, 1), :]
        br = brt_ref[g]
        bi = bit_ref[g]
        rows = slice(gl * GROUP_W, (gl + 1) * GROUP_W)
        cols = slice(gl * N_STATE, (gl + 1) * N_STATE)
        cols_im = slice(BLOCK_STATE + gl * N_STATE, BLOCK_STATE + (gl + 1) * N_STATE)
        bblk_ref[j, rows, cols] = zr * br - zi * bi
        bblk_ref[j, rows, cols_im] = zr * bi + zi * br
        cblk_ref[j, cols, rows] = crt_ref[g]
        cblk_ref[j, cols_im, rows] = -cit_ref[g]


def _discretise(lam_re, lam_im, log_step, b_re, b_im, c_re, c_im):
    f32 = jnp.float32
    brt = jnp.transpose(b_re, (0, 2, 1))
    bit = jnp.transpose(b_im, (0, 2, 1))
    crt = jnp.transpose(c_re, (0, 2, 1))
    cit = jnp.transpose(c_im, (0, 2, 1))
    out_shape = (
        jax.ShapeDtypeStruct((N_GROUPS, N_STATE), f32),
        jax.ShapeDtypeStruct((N_GROUPS, N_STATE), f32),
        jax.ShapeDtypeStruct((N_GROUPS, N_STATE), f32),
        jax.ShapeDtypeStruct((N_BLOCKS, BLOCK_IN, 2 * BLOCK_STATE), f32),
        jax.ShapeDtypeStruct((N_BLOCKS, 2 * BLOCK_STATE, BLOCK_IN), f32),
    )
    return pl.pallas_call(
        _discretise_kernel,
        out_shape=out_shape,
        scratch_shapes=[pltpu.VMEM((N_GROUPS, N_STATE), f32), pltpu.VMEM((N_GROUPS, N_STATE), f32)],
        compiler_params=pltpu.CompilerParams(vmem_limit_bytes=VMEM_LIMIT),
        name="discretise",
    )(lam_re, lam_im, log_step.reshape(N_GROUPS, 1), brt, bit, crt, cit)


def _mixer_kernel(x_ref, h0_ref, c0_ref, g_ref, win_ref, a1_ref, a2_ref, bblk_ref, cblk_ref, d_ref,
                  wglu_ref, bglu_ref, wconv_ref, bconv_ref,
                  mix_ref, hout_ref, cout_ref,
                  bu_ref, perm_ref, vbuf_ref, xn_ref, *, tt):
    f32, bf16 = jnp.float32, jnp.bfloat16
    nseq = SEQ_PER_STEP
    m = nseq * tt
    nth = tt // SUBLANES
    nq = tt // QUAD
    half = nseq * QUAD
    ti = pl.program_id(1)

    @pl.when(ti == 0)
    def _():
        hout_ref[...] = h0_ref[...]
        vbuf_ref[:, 0:SUBLANES, :] = c0_ref[...]

    x = x_ref[...].reshape(m, D_MODEL)
    r = lax.rsqrt(jnp.mean(x * x, axis=-1, keepdims=True) + EPS)
    xn_ref[...] = ((x * r) * g_ref[...]).astype(bf16)

    u = jnp.dot(xn_ref[...], win_ref[:, 0:D_SSM], preferred_element_type=f32)

    low_rows = lax.broadcasted_iota(jnp.int32, (SUBLANES, D_SSM), 0) < QUAD

    def halves(first, second, high):
        if high:
            return jnp.where(low_rows, pltpu.roll(first, QUAD, axis=0), second)
        return jnp.where(low_rows, first, pltpu.roll(second, QUAD, axis=0))

    u4 = u.reshape(nseq, nth, SUBLANES, D_SSM)
    for th in range(nth):
        for high in (False, True):
            for pr in range(nseq // 2):
                perm_ref[2 * th + high, pr] = halves(u4[2 * pr, th], u4[2 * pr + 1, th], high)
    up = perm_ref[...].reshape(m, D_SSM).astype(bf16)

    sections = 4
    steps = tt // sections
    per_gate = D_CONV // GATE_CHUNK
    assert per_gate == N_BLOCKS

    def gate_cols(gate, c):
        start = D_SSM + gate * D_CONV + c * GATE_CHUNK
        return slice(start, start + GATE_CHUNK)

    def project_b(j):
        bu = jnp.dot(up[:, j * BLOCK_IN:(j + 1) * BLOCK_IN], bblk_ref[j], preferred_element_type=f32)
        for k in range(SLABS):
            re = bu[:, k * LANES:(k + 1) * LANES]
            im = bu[:, BLOCK_STATE + k * LANES:BLOCK_STATE + (k + 1) * LANES]
            bu_ref[j % 2, k, :, 0:half, :] = re.reshape(nq, half, LANES)
            bu_ref[j % 2, k, :, half:2 * half, :] = im.reshape(nq, half, LANES)

    def advance(j, q, hs, a1, a2):
        buf = j % 2
        for s in range(q * steps, (q + 1) * steps):
            tq, t4 = divmod(s, QUAD)
            rows = pl.ds(t4, 2 * nseq, stride=QUAD)
            for k in range(SLABS):
                h = hs[k]
                nh = a1[k] * h + a2[k] * pltpu.roll(h, nseq, axis=0) + bu_ref[buf, k, tq, rows, :]
                bu_ref[buf, k, tq, rows, :] = nh
                hs[k] = nh
        return hs

    project_b(0)
    for j in range(N_BLOCKS):
        if j + 1 < N_BLOCKS:
            project_b(j + 1)
        a1 = [a1_ref[j * SLABS + k] for k in range(SLABS)]
        a2 = [a2_ref[j * SLABS + k] for k in range(SLABS)]
        hs = [hout_ref[j * SLABS + k] for k in range(SLABS)]

        cols = slice(j * GATE_CHUNK, (j + 1) * GATE_CHUNK)
        gate_b = jnp.dot(xn_ref[...], win_ref[:, gate_cols(0, j)], preferred_element_type=f32)
        hs = advance(j, 0, hs, a1, a2)
        gate_c = jnp.dot(xn_ref[...], win_ref[:, gate_cols(1, j)], preferred_element_type=f32)
        hs = advance(j, 1, hs, a1, a2)
        xv = jnp.dot(xn_ref[...], win_ref[:, gate_cols(2, j)], preferred_element_type=f32)
        hs = advance(j, 2, hs, a1, a2)
        v = gate_c * xv
        vbuf_ref[:, SUBLANES:, cols] = v.reshape(nseq, tt, GATE_CHUNK)
        v1 = vbuf_ref[:, SUBLANES - 1:SUBLANES - 1 + tt, cols].reshape(m, GATE_CHUNK)
        v2 = vbuf_ref[:, SUBLANES - 2:SUBLANES - 2 + tt, cols].reshape(m, GATE_CHUNK)
        yc = (bconv_ref[:, cols] + wconv_ref[0:1, cols] * v2 + wconv_ref[1:2, cols] * v1
              + wconv_ref[2:3, cols] * v)
        mix_ref[:, :, D_SSM + j * GATE_CHUNK:D_SSM + (j + 1) * GATE_CHUNK] = (
            (gate_b * yc).astype(bf16).reshape(nseq, tt, GATE_CHUNK))
        hs = advance(j, 3, hs, a1, a2)

        for k in range(SLABS):
            hout_ref[j * SLABS + k] = hs[k]
        h_all = jnp.concatenate(
            [bu_ref[j % 2, k, :, 0:half, :].reshape(m, LANES) for k in range(SLABS)]
            + [bu_ref[j % 2, k, :, half:2 * half, :].reshape(m, LANES) for k in range(SLABS)],
            axis=1).astype(bf16)
        yj = jnp.dot(h_all, cblk_ref[j], preferred_element_type=f32)
        ucols = slice(j * BLOCK_IN, (j + 1) * BLOCK_IN)
        perm_ref[:, :, :, ucols] = (yj.reshape(nq, nseq // 2, SUBLANES, BLOCK_IN)
                                    + d_ref[:, ucols] * perm_ref[:, :, :, ucols])

    tail = vbuf_ref[:, tt:tt + SUBLANES, :]
    vbuf_ref[:, 0:SUBLANES, :] = tail
    cout_ref[...] = tail

    y = jnp.concatenate(
        [halves(perm_ref[2 * th, b // 2], perm_ref[2 * th + 1, b // 2], b % 2 == 1)
         for b in range(nseq) for th in range(nth)], axis=0)
    y = 0.5 * y * (1.0 + lax.erf(y * INV_SQRT2))
    gl = jnp.dot(y.astype(bf16), wglu_ref[...], preferred_element_type=f32) + bglu_ref[...]
    y_ssm = y * jax.nn.sigmoid(gl)
    mix_ref[:, :, 0:D_SSM] = y_ssm.astype(bf16).reshape(nseq, tt, D_SSM)


def _const_spec(shape):
    zeros = (0,) * len(shape)
    return pl.BlockSpec(shape, lambda *_: zeros, pipeline_mode=pl.Buffered(1))


def _mixer(x, h0, c0, norm_g, w_in, a1_tab, a2_tab, bblk, cblk, d, w_glu, b_glu, w_conv, b_conv, *, tt):
    nbg, nseq, seqlen, _ = x.shape
    assert nseq == SEQ_PER_STEP and seqlen % tt == 0 and tt % (4 * SUBLANES) == 0
    nt = seqlen // tt
    m = nseq * tt
    f32 = jnp.float32
    nstate_rows = N_BLOCKS * SLABS
    in_specs = [
        pl.BlockSpec((None, nseq, tt, D_MODEL), lambda bg, ti: (bg, 0, ti, 0)),
        pl.BlockSpec((None, nstate_rows, SUBLANES, LANES), lambda bg, ti: (bg, 0, 0, 0)),
        pl.BlockSpec((None, nseq, SUBLANES, D_CONV), lambda bg, ti: (bg, 0, 0, 0)),
        _const_spec((1, D_MODEL)),
        _const_spec((D_MODEL, D_IN)),
        _const_spec((nstate_rows, SUBLANES, LANES)),
        _const_spec((nstate_rows, SUBLANES, LANES)),
        _const_spec((N_BLOCKS, BLOCK_IN, 2 * BLOCK_STATE)),
        _const_spec((N_BLOCKS, 2 * BLOCK_STATE, BLOCK_IN)),
        _const_spec((1, D_SSM)),
        _const_spec((D_SSM, D_SSM)),
        _const_spec((1, D_SSM)),
        _const_spec((CONV_WIDTH, D_CONV)),
        _const_spec((1, D_CONV)),
    ]
    out_specs = [
        pl.BlockSpec((None, nseq, tt, D_MODEL), lambda bg, ti: (bg, 0, ti, 0)),
        pl.BlockSpec((None, nstate_rows, SUBLANES, LANES), lambda bg, ti: (bg, 0, 0, 0)),
        pl.BlockSpec((None, nseq, SUBLANES, D_CONV), lambda bg, ti: (bg, 0, 0, 0)),
    ]
    out_shape = (
        jax.ShapeDtypeStruct((nbg, nseq, seqlen, D_MODEL), jnp.bfloat16),
        jax.ShapeDtypeStruct((nbg, nstate_rows, SUBLANES, LANES), f32),
        jax.ShapeDtypeStruct((nbg, nseq, SUBLANES, D_CONV), f32),
    )
    scratch = [
        pltpu.VMEM((2, SLABS, tt // QUAD, QUAD_ROWS, LANES), f32),
        pltpu.VMEM((tt // QUAD, nseq // 2, SUBLANES, D_SSM), f32),
        pltpu.VMEM((nseq, tt + SUBLANES, D_CONV), f32),
        pltpu.VMEM((m, D_MODEL), jnp.bfloat16),
    ]
    return pl.pallas_call(
        functools.partial(_mixer_kernel, tt=tt),
        grid=(nbg, nt),
        in_specs=in_specs,
        out_specs=out_specs,
        out_shape=out_shape,
        scratch_shapes=scratch,
        compiler_params=pltpu.CompilerParams(
            dimension_semantics=("arbitrary", "arbitrary"), vmem_limit_bytes=VMEM_LIMIT),
        name="mixer",
    )(x, h0, c0, norm_g, w_in, a1_tab, a2_tab, bblk, cblk, d, w_glu, b_glu, w_conv, b_conv)


def _mlp_kernel(x_ref, mix_ref, wout_ref, g_ref, wup_ref, wdown_ref, gfin_ref, o_ref, xn_ref):
    f32, bf16 = jnp.float32, jnp.bfloat16
    f = pl.program_id(1)

    @pl.when(f == 0)
    def _():
        x1 = x_ref[...] + jnp.dot(mix_ref[...], wout_ref[...], preferred_element_type=f32)
        o_ref[...] = x1
        r = lax.rsqrt(jnp.mean(x1 * x1, axis=-1, keepdims=True) + EPS)
        xn_ref[...] = ((x1 * r) * g_ref[...]).astype(bf16)

    hm = jnp.dot(xn_ref[...], wup_ref[...], preferred_element_type=f32)
    act = jnp.square(jnp.maximum(hm, 0.0)).astype(bf16)
    o_ref[...] += jnp.dot(act, wdown_ref[...], preferred_element_type=f32)

    @pl.when(f == pl.num_programs(1) - 1)
    def _():
        x2 = o_ref[...]
        r = lax.rsqrt(jnp.mean(x2 * x2, axis=-1, keepdims=True) + EPS)
        o_ref[...] = (x2 * r) * gfin_ref[...]


def _mlp(x, mix, w_out, norm_g, w_up, w_down, norm_final_g, *, tm, tf):
    tokens = x.shape[0]
    assert tokens % tm == 0 and D_FF % tf == 0
    in_specs = [
        pl.BlockSpec((tm, D_MODEL), lambda i, f: (i, 0)),
        pl.BlockSpec((tm, D_MODEL), lambda i, f: (i, 0)),
        _const_spec((D_MODEL, D_MODEL)),
        _const_spec((1, D_MODEL)),
        pl.BlockSpec((D_MODEL, tf), lambda i, f: (0, f)),
        pl.BlockSpec((tf, D_MODEL), lambda i, f: (f, 0)),
        _const_spec((1, D_MODEL)),
    ]
    return pl.pallas_call(
        _mlp_kernel,
        grid=(tokens // tm, D_FF // tf),
        in_specs=in_specs,
        out_specs=pl.BlockSpec((tm, D_MODEL), lambda i, f: (i, 0)),
        out_shape=jax.ShapeDtypeStruct((tokens, D_MODEL), jnp.float32),
        scratch_shapes=[pltpu.VMEM((tm, D_MODEL), jnp.bfloat16)],
        compiler_params=pltpu.CompilerParams(
            dimension_semantics=("arbitrary", "arbitrary"), vmem_limit_bytes=VMEM_LIMIT),
        name="mlp",
    )(x, mix, w_out, norm_g, w_up, w_down, norm_final_g)


def _mlp_cast_kernel(x_ref, mix_ref, wout_ref, g_ref, wup_ref, wdown_ref, gfin_ref,
                     o_ref, wup_bf_ref, wdown_bf_ref, xn_ref):
    f32, bf16 = jnp.float32, jnp.bfloat16
    f = pl.program_id(0)
    wup = wup_ref[...].astype(bf16)
    wdown = wdown_ref[...].astype(bf16)
    wup_bf_ref[...] = wup
    wdown_bf_ref[...] = wdown

    @pl.when(f == 0)
    def _():
        x1 = x_ref[...] + jnp.dot(mix_ref[...], wout_ref[...], preferred_element_type=f32)
        o_ref[...] = x1
        r = lax.rsqrt(jnp.mean(x1 * x1, axis=-1, keepdims=True) + EPS)
        xn_ref[...] = ((x1 * r) * g_ref[...]).astype(bf16)

    hm = jnp.dot(xn_ref[...], wup, preferred_element_type=f32)
    act = jnp.square(jnp.maximum(hm, 0.0)).astype(bf16)
    o_ref[...] += jnp.dot(act, wdown, preferred_element_type=f32)

    @pl.when(f == pl.num_programs(0) - 1)
    def _():
        x2 = o_ref[...]
        r = lax.rsqrt(jnp.mean(x2 * x2, axis=-1, keepdims=True) + EPS)
        o_ref[...] = (x2 * r) * gfin_ref[...]


def _mlp_cast(x, mix, w_out, norm_g, w_up_f32, w_down_f32, norm_final_g, *, tf):
    tokens = x.shape[0]
    assert D_FF % tf == 0
    bf16 = jnp.bfloat16
    in_specs = [
        _const_spec((tokens, D_MODEL)),
        _const_spec((tokens, D_MODEL)),
        _const_spec((D_MODEL, D_MODEL)),
        _const_spec((1, D_MODEL)),
        pl.BlockSpec((D_MODEL, tf), lambda f: (0, f)),
        pl.BlockSpec((tf, D_MODEL), lambda f: (f, 0)),
        _const_spec((1, D_MODEL)),
    ]
    out_specs = [
        pl.BlockSpec((tokens, D_MODEL), lambda f: (0, 0)),
        pl.BlockSpec((D_MODEL, tf), lambda f: (0, f)),
        pl.BlockSpec((tf, D_MODEL), lambda f: (f, 0)),
    ]
    out_shape = (
        jax.ShapeDtypeStruct((tokens, D_MODEL), jnp.float32),
        jax.ShapeDtypeStruct((D_MODEL, D_FF), bf16),
        jax.ShapeDtypeStruct((D_FF, D_MODEL), bf16),
    )
    return pl.pallas_call(
        _mlp_cast_kernel,
        grid=(D_FF // tf,),
        in_specs=in_specs,
        out_specs=out_specs,
        out_shape=out_shape,
        scratch_shapes=[pltpu.VMEM((tokens, D_MODEL), bf16)],
        compiler_params=pltpu.CompilerParams(
            dimension_semantics=("arbitrary",), vmem_limit_bytes=VMEM_LIMIT),
        name="mlp_cast",
    )(x, mix, w_out, norm_g, w_up_f32, w_down_f32, norm_final_g)


def _state_to_rows(h_re, h_im):
    def one(h):
        nb = h.shape[0]
        h = h.reshape(nb // SEQ_PER_STEP, SEQ_PER_STEP, N_BLOCKS, SLABS, LANES)
        return jnp.transpose(h, (0, 2, 3, 1, 4))
    both = jnp.concatenate([one(h_re), one(h_im)], axis=3)
    return both.reshape(both.shape[0], N_BLOCKS * SLABS, 2 * SEQ_PER_STEP, LANES)


def _rows_to_state(rows):
    nbg = rows.shape[0]
    r = rows.reshape(nbg, N_BLOCKS, SLABS, 2, SEQ_PER_STEP, LANES)
    r = jnp.transpose(r, (3, 0, 4, 1, 2, 5))
    r = r.reshape(2, nbg * SEQ_PER_STEP, N_GROUPS, N_STATE)
    return r[0], r[1]


def _a_tables(a_re, a_im, neg_a_im):
    def rows(t, n):
        return jnp.broadcast_to(t.reshape(N_BLOCKS * SLABS, 1, LANES), (N_BLOCKS * SLABS, n, LANES))
    a1 = rows(a_re, 2 * SEQ_PER_STEP)
    a2 = jnp.concatenate([rows(neg_a_im, SEQ_PER_STEP), rows(a_im, SEQ_PER_STEP)], axis=1)
    return a1, a2


def _run_mixer(x, h_re, h_im, conv_prev, lw, tabs, *, tt):
    bsz, seqlen, _ = x.shape
    nbg = bsz // SEQ_PER_STEP
    x4 = x.reshape(nbg, SEQ_PER_STEP, seqlen, D_MODEL)
    h0 = _state_to_rows(h_re, h_im)
    c0 = jnp.pad(conv_prev, ((0, 0), (SUBLANES - (CONV_WIDTH - 1), 0), (0, 0)))
    c0 = c0.reshape(nbg, SEQ_PER_STEP, SUBLANES, D_CONV)
    mix, hout, cout = _mixer(x4, h0, c0, lw["norm_mix_g"], lw["w_in"], *tabs, lw["d"],
                             lw["w_glu"], lw["b_glu"], lw["w_conv"], lw["b_conv"], tt=tt)
    new_re, new_im = _rows_to_state(hout)
    new_conv = cout.reshape(bsz, SUBLANES, D_CONV)[:, SUBLANES - (CONV_WIDTH - 1):, :]
    return mix.reshape(bsz * seqlen, D_MODEL), new_re, new_im, new_conv


def kernel(x_prompt, x_sample, state_ssm_re, state_ssm_im, cache_conv, norm_mix_g, w_in, ssm_lambda_re, ssm_lambda_im, ssm_log_step, ssm_b_re, ssm_b_im, ssm_c_re, ssm_c_im, ssm_d, w_glu, b_glu, w_conv, b_conv, w_out, norm_mlp_g, w_up, w_down, norm_final_g):
    depth = w_in.shape[0]
    assert depth == 1, "the final norm is fused into the last (only) layer"
    bf16 = jnp.bfloat16
    bp = x_prompt.shape[0]
    a_re, a_im, neg_a_im, bblk, cblk = _discretise(
        ssm_lambda_re[0], ssm_lambda_im[0], ssm_log_step[0], ssm_b_re[0], ssm_b_im[0], ssm_c_re[0], ssm_c_im[0])
    tabs = _a_tables(a_re, a_im, neg_a_im) + (bblk.astype(bf16), cblk.astype(bf16))
    lw = dict(
        norm_mix_g=norm_mix_g[0].reshape(1, D_MODEL), w_in=w_in[0].astype(bf16),
        d=ssm_d[0].reshape(1, D_SSM), w_glu=w_glu[0].astype(bf16), b_glu=b_glu[0].reshape(1, D_SSM),
        w_conv=w_conv[0], b_conv=b_conv[0].reshape(1, D_CONV), w_out=w_out[0].astype(bf16),
        norm_mlp_g=norm_mlp_g[0].reshape(1, D_MODEL))
    gfin = norm_final_g.reshape(1, D_MODEL)

    zeros_h = jnp.zeros((bp, N_GROUPS, N_STATE), jnp.float32)
    zeros_c = jnp.zeros((bp, CONV_WIDTH - 1, D_CONV), x_prompt.dtype)
    mix_s, sre, sim, scv = _run_mixer(x_sample, state_ssm_re[0], state_ssm_im[0], cache_conv[0], lw, tabs, tt=64)
    mix_p, pre, pim, pcv = _run_mixer(x_prompt, zeros_h, zeros_h, zeros_c, lw, tabs, tt=128)
    ys, w_up_bf, w_down_bf = _mlp_cast(x_sample.reshape(-1, D_MODEL), mix_s, lw["w_out"], lw["norm_mlp_g"],
                                       w_up.reshape(D_MODEL, D_FF), w_down.reshape(D_FF, D_MODEL), gfin, tf=512)
    yp = _mlp(x_prompt.reshape(-1, D_MODEL), mix_p, lw["w_out"], lw["norm_mlp_g"], w_up_bf, w_down_bf, gfin,
              tm=512, tf=1024)
    return (yp.reshape(x_prompt.shape), ys.reshape(x_sample.shape), pre[None], pim[None], pcv[None],
            sre[None], sim[None], scv[None])
```

```python
import functools

import jax
import jax.numpy as jnp
from jax import lax
from jax.experimental import pallas as pl
from jax.experimental.pallas import tpu as pltpu

D_MODEL = 2048
D_SSM = 1024
D_CONV = 1024
N_GROUPS = 64
GROUP_W = 16
N_STATE = 64
D_FF = 8192
D_IN = D_SSM + 3 * D_CONV
CONV_WIDTH = 3
EPS = 1e-6
INV_SQRT2 = 0.7071067811865476

SUBLANES = 8
LANES = 128
SEQ_PER_STEP = 4
GROUPS_PER_BLOCK = 16
N_BLOCKS = N_GROUPS // GROUPS_PER_BLOCK
BLOCK_IN = GROUPS_PER_BLOCK * GROUP_W
BLOCK_STATE = GROUPS_PER_BLOCK * N_STATE
SLABS = BLOCK_STATE // LANES
GATE_CHUNK = 256
QUAD = 4
QUAD_ROWS = 2 * SEQ_PER_STEP * QUAD
VMEM_LIMIT = 60 * 1024 * 1024


def _discretise_kernel(lre_ref, lim_ref, ls_ref, brt_ref, bit_ref, crt_ref, cit_ref,
                       are_ref, aim_ref, naim_ref, bblk_out_ref, cblk_out_ref, zr_ref, zi_ref, bblk_ref, cblk_ref):
    lre = lre_ref[...]
    lim = lim_ref[...]
    delta = jnp.exp(ls_ref[...])
    mag = jnp.exp(lre * delta)
    a_re = mag * jnp.cos(lim * delta)
    a_im = mag * jnp.sin(lim * delta)
    den = lre * lre + lim * lim
    zr_ref[...] = ((a_re - 1.0) * lre + a_im * lim) / den
    zi_ref[...] = (a_im * lre - (a_re - 1.0) * lim) / den
    are_ref[...] = a_re
    aim_ref[...] = a_im
    naim_ref[...] = -a_im
    bblk_ref[...] = jnp.zeros(bblk_ref.shape, jnp.float32)
    cblk_ref[...] = jnp.zeros(cblk_ref.shape, jnp.float32)
    for g in range(N_GROUPS):
        j, gl = divmod(g, GROUPS_PER_BLOCK)
        zr = zr_ref[pl.ds(g, 1), :]
        zi = zi_ref[pl.ds(g, 1), :]
        br = brt_ref[g]
        bi = bit_ref[g]
        rows = slice(gl * GROUP_W, (gl + 1) * GROUP_W)
        cols = slice(gl * N_STATE, (gl + 1) * N_STATE)
        cols_im = slice(BLOCK_STATE + gl * N_STATE, BLOCK_STATE + (gl + 1) * N_STATE)
        bblk_ref[j, rows, cols] = zr * br - zi * bi
        bblk_ref[j, rows, cols_im] = zr * bi + zi * br
        cblk_ref[j, cols, rows] = crt_ref[g]
        cblk_ref[j, cols_im, rows] = -cit_ref[g]
    bblk_out_ref[...] = bblk_ref[...].astype(jnp.bfloat16)
    cblk_out_ref[...] = cblk_ref[...].astype(jnp.bfloat16)


def _discretise(lam_re, lam_im, log_step, b_re, b_im, c_re, c_im):
    f32 = jnp.float32
    brt = jnp.transpose(b_re, (0, 2, 1))
    bit = jnp.transpose(b_im, (0, 2, 1))
    crt = jnp.transpose(c_re, (0, 2, 1))
    cit = jnp.transpose(c_im, (0, 2, 1))
    out_shape = (
        jax.ShapeDtypeStruct((N_GROUPS, N_STATE), f32),
        jax.ShapeDtypeStruct((N_GROUPS, N_STATE), f32),
        jax.ShapeDtypeStruct((N_GROUPS, N_STATE), f32),
        jax.ShapeDtypeStruct((N_BLOCKS, BLOCK_IN, 2 * BLOCK_STATE), jnp.bfloat16),
        jax.ShapeDtypeStruct((N_BLOCKS, 2 * BLOCK_STATE, BLOCK_IN), jnp.bfloat16),
    )
    return pl.pallas_call(
        _discretise_kernel,
        out_shape=out_shape,
        scratch_shapes=[pltpu.VMEM((N_GROUPS, N_STATE), f32), pltpu.VMEM((N_GROUPS, N_STATE), f32),
                        pltpu.VMEM((N_BLOCKS, BLOCK_IN, 2 * BLOCK_STATE), f32),
                        pltpu.VMEM((N_BLOCKS, 2 * BLOCK_STATE, BLOCK_IN), f32)],
        compiler_params=pltpu.CompilerParams(vmem_limit_bytes=VMEM_LIMIT),
        name="discretise",
    )(lam_re, lam_im, log_step.reshape(N_GROUPS, 1), brt, bit, crt, cit)


def _mixer_kernel(x_ref, h0_ref, c0_ref, g_ref, win_ref, a1_ref, a2_ref, bblk_ref, cblk_ref, d_ref,
                  wglu_ref, bglu_ref, wconv_ref, bconv_ref,
                  mix_ref, hout_ref, cout_ref,
                  bu_ref, perm_ref, vbuf_ref, xn_ref, *, tt):
    f32, bf16 = jnp.float32, jnp.bfloat16
    nseq = SEQ_PER_STEP
    m = nseq * tt
    nth = tt // SUBLANES
    nq = tt // QUAD
    half = nseq * QUAD
    ti = pl.program_id(1)

    @pl.when(ti == 0)
    def _():
        hout_ref[...] = h0_ref[...]
        vbuf_ref[:, 0:SUBLANES, :] = c0_ref[...]

    x = x_ref[...].reshape(m, D_MODEL)
    r = lax.rsqrt(jnp.mean(x * x, axis=-1, keepdims=True) + EPS)
    xn_ref[...] = ((x * r) * g_ref[...]).astype(bf16)

    u = jnp.dot(xn_ref[...], win_ref[:, 0:D_SSM], preferred_element_type=f32)

    low_rows = lax.broadcasted_iota(jnp.int32, (SUBLANES, D_SSM), 0) < QUAD

    def halves(first, second, high):
        if high:
            return jnp.where(low_rows, pltpu.roll(first, QUAD, axis=0), second)
        return jnp.where(low_rows, first, pltpu.roll(second, QUAD, axis=0))

    u4 = u.reshape(nseq, nth, SUBLANES, D_SSM)
    for th in range(nth):
        for high in (False, True):
            for pr in range(nseq // 2):
                perm_ref[2 * th + high, pr] = halves(u4[2 * pr, th], u4[2 * pr + 1, th], high)
    up = perm_ref[...].reshape(m, D_SSM).astype(bf16)

    sections = 4
    steps = tt // sections
    per_gate = D_CONV // GATE_CHUNK
    assert per_gate == N_BLOCKS

    def gate_cols(gate, c):
        start = D_SSM + gate * D_CONV + c * GATE_CHUNK
        return slice(start, start + GATE_CHUNK)

    def project_b(j):
        bu = jnp.dot(up[:, j * BLOCK_IN:(j + 1) * BLOCK_IN], bblk_ref[j], preferred_element_type=f32)
        for k in range(SLABS):
            re = bu[:, k * LANES:(k + 1) * LANES]
            im = bu[:, BLOCK_STATE + k * LANES:BLOCK_STATE + (k + 1) * LANES]
            bu_ref[j % 2, k, :, 0:half, :] = re.reshape(nq, half, LANES)
            bu_ref[j % 2, k, :, half:2 * half, :] = im.reshape(nq, half, LANES)

    def advance(j, q, hs, a1, a2):
        buf = j % 2
        for s in range(q * steps, (q + 1) * steps):
            tq, t4 = divmod(s, QUAD)
            rows = pl.ds(t4, 2 * nseq, stride=QUAD)
            for k in range(SLABS):
                h = hs[k]
                nh = a1[k] * h + a2[k] * pltpu.roll(h, nseq, axis=0) + bu_ref[buf, k, tq, rows, :]
                bu_ref[buf, k, tq, rows, :] = nh
                hs[k] = nh
        return hs

    project_b(0)
    for j in range(N_BLOCKS):
        if j + 1 < N_BLOCKS:
            project_b(j + 1)
        a1 = [a1_ref[j * SLABS + k] for k in range(SLABS)]
        a2 = [a2_ref[j * SLABS + k] for k in range(SLABS)]
        hs = [hout_ref[j * SLABS + k] for k in range(SLABS)]

        cols = slice(j * GATE_CHUNK, (j + 1) * GATE_CHUNK)
        gate_b = jnp.dot(xn_ref[...], win_ref[:, gate_cols(0, j)], preferred_element_type=f32)
        hs = advance(j, 0, hs, a1, a2)
        gate_c = jnp.dot(xn_ref[...], win_ref[:, gate_cols(1, j)], preferred_element_type=f32)
        hs = advance(j, 1, hs, a1, a2)
        xv = jnp.dot(xn_ref[...], win_ref[:, gate_cols(2, j)], preferred_element_type=f32)
        hs = advance(j, 2, hs, a1, a2)
        v = gate_c * xv
        vbuf_ref[:, SUBLANES:, cols] = v.reshape(nseq, tt, GATE_CHUNK)
        v1 = vbuf_ref[:, SUBLANES - 1:SUBLANES - 1 + tt, cols].reshape(m, GATE_CHUNK)
        v2 = vbuf_ref[:, SUBLANES - 2:SUBLANES - 2 + tt, cols].reshape(m, GATE_CHUNK)
        yc = (bconv_ref[:, cols] + wconv_ref[0:1, cols] * v2 + wconv_ref[1:2, cols] * v1
              + wconv_ref[2:3, cols] * v)
        mix_ref[:, :, D_SSM + j * GATE_CHUNK:D_SSM + (j + 1) * GATE_CHUNK] = (
            (gate_b * yc).astype(bf16).reshape(nseq, tt, GATE_CHUNK))
        hs = advance(j, 3, hs, a1, a2)

        for k in range(SLABS):
            hout_ref[j * SLABS + k] = hs[k]
        h_all = jnp.concatenate(
            [bu_ref[j % 2, k, :, 0:half, :].reshape(m, LANES) for k in range(SLABS)]
            + [bu_ref[j % 2, k, :, half:2 * half, :].reshape(m, LANES) for k in range(SLABS)],
            axis=1).astype(bf16)
        yj = jnp.dot(h_all, cblk_ref[j], preferred_element_type=f32)
        ucols = slice(j * BLOCK_IN, (j + 1) * BLOCK_IN)
        perm_ref[:, :, :, ucols] = (yj.reshape(nq, nseq // 2, SUBLANES, BLOCK_IN)
                                    + d_ref[:, ucols] * perm_ref[:, :, :, ucols])

    tail = vbuf_ref[:, tt:tt + SUBLANES, :]
    vbuf_ref[:, 0:SUBLANES, :] = tail
    cout_ref[...] = tail

    y = jnp.concatenate(
        [halves(perm_ref[2 * th, b // 2], perm_ref[2 * th + 1, b // 2], b % 2 == 1)
         for b in range(nseq) for th in range(nth)], axis=0)
    y = 0.5 * y * (1.0 + lax.erf(y * INV_SQRT2))
    gl = jnp.dot(y.astype(bf16), wglu_ref[...], preferred_element_type=f32) + bglu_ref[...]
    y_ssm = y * jax.nn.sigmoid(gl)
    mix_ref[:, :, 0:D_SSM] = y_ssm.astype(bf16).reshape(nseq, tt, D_SSM)


def _const_spec(shape):
    zeros = (0,) * len(shape)
    return pl.BlockSpec(shape, lambda *_: zeros, pipeline_mode=pl.Buffered(1))


def _mixer(x, h0, c0, norm_g, w_in, a1_tab, a2_tab, bblk, cblk, d, w_glu, b_glu, w_conv, b_conv, *, tt):
    nbg, nseq, seqlen, _ = x.shape
    assert nseq == SEQ_PER_STEP and seqlen % tt == 0 and tt % (4 * SUBLANES) == 0
    nt = seqlen // tt
    m = nseq * tt
    f32 = jnp.float32
    nstate_rows = N_BLOCKS * SLABS
    in_specs = [
        pl.BlockSpec((None, nseq, tt, D_MODEL), lambda bg, ti: (bg, 0, ti, 0)),
        pl.BlockSpec((None, nstate_rows, SUBLANES, LANES), lambda bg, ti: (bg, 0, 0, 0)),
        pl.BlockSpec((None, nseq, SUBLANES, D_CONV), lambda bg, ti: (bg, 0, 0, 0)),
        _const_spec((1, D_MODEL)),
        _const_spec((D_MODEL, D_IN)),
        _const_spec((nstate_rows, SUBLANES, LANES)),
        _const_spec((nstate_rows, SUBLANES, LANES)),
        _const_spec((N_BLOCKS, BLOCK_IN, 2 * BLOCK_STATE)),
        _const_spec((N_BLOCKS, 2 * BLOCK_STATE, BLOCK_IN)),
        _const_spec((1, D_SSM)),
        _const_spec((D_SSM, D_SSM)),
        _const_spec((1, D_SSM)),
        _const_spec((CONV_WIDTH, D_CONV)),
        _const_spec((1, D_CONV)),
    ]
    out_specs = [
        pl.BlockSpec((None, nseq, tt, D_MODEL), lambda bg, ti: (bg, 0, ti, 0)),
        pl.BlockSpec((None, nstate_rows, SUBLANES, LANES), lambda bg, ti: (bg, 0, 0, 0)),
        pl.BlockSpec((None, nseq, SUBLANES, D_CONV), lambda bg, ti: (bg, 0, 0, 0)),
    ]
    out_shape = (
        jax.ShapeDtypeStruct((nbg, nseq, seqlen, D_MODEL), jnp.bfloat16),
        jax.ShapeDtypeStruct((nbg, nstate_rows, SUBLANES, LANES), f32),
        jax.ShapeDtypeStruct((nbg, nseq, SUBLANES, D_CONV), f32),
    )
    scratch = [
        pltpu.VMEM((2, SLABS, tt // QUAD, QUAD_ROWS, LANES), f32),
        pltpu.VMEM((tt // QUAD, nseq // 2, SUBLANES, D_SSM), f32),
        pltpu.VMEM((nseq, tt + SUBLANES, D_CONV), f32),
        pltpu.VMEM((m, D_MODEL), jnp.bfloat16),
    ]
    return pl.pallas_call(
        functools.partial(_mixer_kernel, tt=tt),
        grid=(nbg, nt),
        in_specs=in_specs,
        out_specs=out_specs,
        out_shape=out_shape,
        scratch_shapes=scratch,
        compiler_params=pltpu.CompilerParams(
            dimension_semantics=("arbitrary", "arbitrary"), vmem_limit_bytes=VMEM_LIMIT),
        name="mixer",
    )(x, h0, c0, norm_g, w_in, a1_tab, a2_tab, bblk, cblk, d, w_glu, b_glu, w_conv, b_conv)


def _mlp_kernel(x_ref, mix_ref, wout_ref, g_ref, wup_ref, wdown_ref, gfin_ref, o_ref, xn_ref):
    f32, bf16 = jnp.float32, jnp.bfloat16
    f = pl.program_id(1)

    @pl.when(f == 0)
    def _():
        x1 = x_ref[...] + jnp.dot(mix_ref[...], wout_ref[...], preferred_element_type=f32)
        o_ref[...] = x1
        r = lax.rsqrt(jnp.mean(x1 * x1, axis=-1, keepdims=True) + EPS)
        xn_ref[...] = ((x1 * r) * g_ref[...]).astype(bf16)

    hm = jnp.dot(xn_ref[...], wup_ref[...], preferred_element_type=f32)
    act = jnp.square(jnp.maximum(hm, 0.0)).astype(bf16)
    o_ref[...] += jnp.dot(act, wdown_ref[...], preferred_element_type=f32)

    @pl.when(f == pl.num_programs(1) - 1)
    def _():
        x2 = o_ref[...]
        r = lax.rsqrt(jnp.mean(x2 * x2, axis=-1, keepdims=True) + EPS)
        o_ref[...] = (x2 * r) * gfin_ref[...]


def _mlp(x, mix, w_out, norm_g, w_up, w_down, norm_final_g, *, tm):
    tokens = x.shape[0]
    tf = w_up.shape[2]
    assert tokens % tm == 0 and w_up.shape == (D_FF // tf, D_MODEL, tf)
    in_specs = [
        pl.BlockSpec((tm, D_MODEL), lambda i, f: (i, 0)),
        pl.BlockSpec((tm, D_MODEL), lambda i, f: (i, 0)),
        _const_spec((D_MODEL, D_MODEL)),
        _const_spec((1, D_MODEL)),
        pl.BlockSpec((None, D_MODEL, tf), lambda i, f: (f, 0, 0)),
        pl.BlockSpec((tf, D_MODEL), lambda i, f: (f, 0)),
        _const_spec((1, D_MODEL)),
    ]
    return pl.pallas_call(
        _mlp_kernel,
        grid=(tokens // tm, D_FF // tf),
        in_specs=in_specs,
        out_specs=pl.BlockSpec((tm, D_MODEL), lambda i, f: (i, 0)),
        out_shape=jax.ShapeDtypeStruct((tokens, D_MODEL), jnp.float32),
        scratch_shapes=[pltpu.VMEM((tm, D_MODEL), jnp.bfloat16)],
        compiler_params=pltpu.CompilerParams(
            dimension_semantics=("arbitrary", "arbitrary"), vmem_limit_bytes=VMEM_LIMIT),
        name="mlp",
    )(x, mix, w_out, norm_g, w_up, w_down, norm_final_g)


def _mlp_cast_kernel(x_ref, mix_ref, wout_ref, g_ref, wup_ref, wdown_ref, gfin_ref,
                     o_ref, wup_bf_ref, wdown_bf_ref, xn_ref):
    f32, bf16 = jnp.float32, jnp.bfloat16
    f = pl.program_id(0)
    wup = wup_ref[...].astype(bf16)
    wdown = wdown_ref[...].astype(bf16)
    wup_bf_ref[...] = wup
    wdown_bf_ref[...] = wdown

    @pl.when(f == 0)
    def _():
        x1 = x_ref[...] + jnp.dot(mix_ref[...], wout_ref[...], preferred_element_type=f32)
        o_ref[...] = x1
        r = lax.rsqrt(jnp.mean(x1 * x1, axis=-1, keepdims=True) + EPS)
        xn_ref[...] = ((x1 * r) * g_ref[...]).astype(bf16)

    hm = jnp.dot(xn_ref[...], wup, preferred_element_type=f32)
    act = jnp.square(jnp.maximum(hm, 0.0)).astype(bf16)
    o_ref[...] += jnp.dot(act, wdown, preferred_element_type=f32)

    @pl.when(f == pl.num_programs(0) - 1)
    def _():
        x2 = o_ref[...]
        r = lax.rsqrt(jnp.mean(x2 * x2, axis=-1, keepdims=True) + EPS)
        o_ref[...] = (x2 * r) * gfin_ref[...]


def _mlp_cast(x, mix, w_out, norm_g, w_up_f32, w_down_f32, norm_final_g, *, tf, tf_out):
    tokens = x.shape[0]
    assert D_FF % tf_out == 0 and tf_out % tf == 0
    per_tile = tf_out // tf
    bf16 = jnp.bfloat16
    in_specs = [
        _const_spec((tokens, D_MODEL)),
        _const_spec((tokens, D_MODEL)),
        _const_spec((D_MODEL, D_MODEL)),
        _const_spec((1, D_MODEL)),
        pl.BlockSpec((D_MODEL, tf), lambda f: (0, f)),
        pl.BlockSpec((tf, D_MODEL), lambda f: (f, 0)),
        _const_spec((1, D_MODEL)),
    ]
    out_specs = [
        pl.BlockSpec((tokens, D_MODEL), lambda f: (0, 0)),
        pl.BlockSpec((None, D_MODEL, tf), lambda f: (f // per_tile, 0, f % per_tile)),
        pl.BlockSpec((tf, D_MODEL), lambda f: (f, 0)),
    ]
    out_shape = (
        jax.ShapeDtypeStruct((tokens, D_MODEL), jnp.float32),
        jax.ShapeDtypeStruct((D_FF // tf_out, D_MODEL, tf_out), bf16),
        jax.ShapeDtypeStruct((D_FF, D_MODEL), bf16),
    )
    return pl.pallas_call(
        _mlp_cast_kernel,
        grid=(D_FF // tf,),
        in_specs=in_specs,
        out_specs=out_specs,
        out_shape=out_shape,
        scratch_shapes=[pltpu.VMEM((tokens, D_MODEL), bf16)],
        compiler_params=pltpu.CompilerParams(
            dimension_semantics=("arbitrary",), vmem_limit_bytes=VMEM_LIMIT),
        name="mlp_cast",
    )(x, mix, w_out, norm_g, w_up_f32, w_down_f32, norm_final_g)


def _state_to_rows(h_re, h_im):
    def one(h):
        nb = h.shape[0]
        h = h.reshape(nb // SEQ_PER_STEP, SEQ_PER_STEP, N_BLOCKS, SLABS, LANES)
        return jnp.transpose(h, (0, 2, 3, 1, 4))
    both = jnp.concatenate([one(h_re), one(h_im)], axis=3)
    return both.reshape(both.shape[0], N_BLOCKS * SLABS, 2 * SEQ_PER_STEP, LANES)


def _rows_to_state(rows):
    nbg = rows.shape[0]
    r = rows.reshape(nbg, N_BLOCKS, SLABS, 2, SEQ_PER_STEP, LANES)
    r = jnp.transpose(r, (3, 0, 4, 1, 2, 5))
    r = r.reshape(2, nbg * SEQ_PER_STEP, N_GROUPS, N_STATE)
    return r[0], r[1]


def _a_tables(a_re, a_im, neg_a_im):
    def rows(t, n):
        return jnp.broadcast_to(t.reshape(N_BLOCKS * SLABS, 1, LANES), (N_BLOCKS * SLABS, n, LANES))
    a1 = rows(a_re, 2 * SEQ_PER_STEP)
    a2 = jnp.concatenate([rows(neg_a_im, SEQ_PER_STEP), rows(a_im, SEQ_PER_STEP)], axis=1)
    return a1, a2


def _run_mixer(x, h_re, h_im, conv_prev, lw, tabs, *, tt):
    bsz, seqlen, _ = x.shape
    nbg = bsz // SEQ_PER_STEP
    x4 = x.reshape(nbg, SEQ_PER_STEP, seqlen, D_MODEL)
    h0 = _state_to_rows(h_re, h_im)
    c0 = jnp.pad(conv_prev, ((0, 0), (SUBLANES - (CONV_WIDTH - 1), 0), (0, 0)))
    c0 = c0.reshape(nbg, SEQ_PER_STEP, SUBLANES, D_CONV)
    mix, hout, cout = _mixer(x4, h0, c0, lw["norm_mix_g"], lw["w_in"], *tabs, lw["d"],
                             lw["w_glu"], lw["b_glu"], lw["w_conv"], lw["b_conv"], tt=tt)
    new_re, new_im = _rows_to_state(hout)
    new_conv = cout.reshape(bsz, SUBLANES, D_CONV)[:, SUBLANES - (CONV_WIDTH - 1):, :]
    return mix.reshape(bsz * seqlen, D_MODEL), new_re, new_im, new_conv


def kernel(x_prompt, x_sample, state_ssm_re, state_ssm_im, cache_conv, norm_mix_g, w_in, ssm_lambda_re, ssm_lambda_im, ssm_log_step, ssm_b_re, ssm_b_im, ssm_c_re, ssm_c_im, ssm_d, w_glu, b_glu, w_conv, b_conv, w_out, norm_mlp_g, w_up, w_down, norm_final_g):
    depth = w_in.shape[0]
    assert depth == 1, "the final norm is fused into the last (only) layer"
    bf16 = jnp.bfloat16
    bp = x_prompt.shape[0]
    a_re, a_im, neg_a_im, bblk, cblk = _discretise(
        ssm_lambda_re[0], ssm_lambda_im[0], ssm_log_step[0], ssm_b_re[0], ssm_b_im[0], ssm_c_re[0], ssm_c_im[0])
    tabs = _a_tables(a_re, a_im, neg_a_im) + (bblk, cblk)
    lw = dict(
        norm_mix_g=norm_mix_g[0].reshape(1, D_MODEL), w_in=w_in[0].astype(bf16),
        d=ssm_d[0].reshape(1, D_SSM), w_glu=w_glu[0].astype(bf16), b_glu=b_glu[0].reshape(1, D_SSM),
        w_conv=w_conv[0], b_conv=b_conv[0].reshape(1, D_CONV), w_out=w_out[0].astype(bf16),
        norm_mlp_g=norm_mlp_g[0].reshape(1, D_MODEL))
    gfin = norm_final_g.reshape(1, D_MODEL)

    zeros_h = jnp.zeros((bp, N_GROUPS, N_STATE), jnp.float32)
    zeros_c = jnp.zeros((bp, CONV_WIDTH - 1, D_CONV), x_prompt.dtype)
    mix_s, sre, sim, scv = _run_mixer(x_sample, state_ssm_re[0], state_ssm_im[0], cache_conv[0], lw, tabs, tt=64)
    mix_p, pre, pim, pcv = _run_mixer(x_prompt, zeros_h, zeros_h, zeros_c, lw, tabs, tt=128)
    ys, w_up_bf, w_down_bf = _mlp_cast(x_sample.reshape(-1, D_MODEL), mix_s, lw["w_out"], lw["norm_mlp_g"],
                                       w_up.reshape(D_MODEL, D_FF), w_down.reshape(D_FF, D_MODEL), gfin, tf=512, tf_out=1024)
    yp = _mlp(x_prompt.reshape(-1, D_MODEL), mix_p, lw["w_out"], lw["norm_mlp_g"], w_up_bf, w_down_bf, gfin,
              tm=512)
    return (yp.reshape(x_prompt.shape), ys.reshape(x_sample.shape), pre[None], pim[None], pcv[None],
            sre[None], sim[None], scv[None])
```

```python
import functools

import jax
import jax.numpy as jnp
from jax import lax
from jax.experimental import pallas as pl
from jax.experimental.pallas import tpu as pltpu

D_MODEL = 2048
D_SSM = 1024
D_CONV = 1024
N_GROUPS = 64
GROUP_W = 16
N_STATE = 64
D_FF = 8192
D_IN = D_SSM + 3 * D_CONV
CONV_WIDTH = 3
EPS = 1e-6
INV_SQRT2 = 0.7071067811865476

SUBLANES = 8
LANES = 128
SEQ_PER_STEP = 4
GROUPS_PER_BLOCK = 16
N_BLOCKS = N_GROUPS // GROUPS_PER_BLOCK
BLOCK_IN = GROUPS_PER_BLOCK * GROUP_W
BLOCK_STATE = GROUPS_PER_BLOCK * N_STATE
SLABS = BLOCK_STATE // LANES
GATE_CHUNK = 256
QUAD = 4
QUAD_ROWS = 2 * SEQ_PER_STEP * QUAD
CH_BLOCK = LANES
N_CH_BLOCKS = D_SSM // CH_BLOCK
GROUPS_PER_CH_BLOCK = CH_BLOCK // GROUP_W
VMEM_LIMIT = 60 * 1024 * 1024


def _discretise_kernel(lre_ref, lim_ref, ls_ref, lrec_ref, limc_ref, brt_ref, bit_ref, cre_ref, cim_ref,
                       crt_ref, cit_ref,
                       a2re_ref, a2im_ref, na2im_ref, wb_out_ref, cc_out_ref, k0_out_ref,
                       zr_ref, zi_ref, are_ref, aim_ref, wb_ref, cc_ref, k0_ref):
    f32 = jnp.float32
    lre = lre_ref[...]
    lim = lim_ref[...]
    delta = jnp.exp(ls_ref[...])
    mag = jnp.exp(lre * delta)
    a_re = mag * jnp.cos(lim * delta)
    a_im = mag * jnp.sin(lim * delta)
    den = lre * lre + lim * lim
    zr_ref[...] = ((a_re - 1.0) * lre + a_im * lim) / den
    zi_ref[...] = (a_im * lre - (a_re - 1.0) * lim) / den
    are_ref[...] = a_re
    aim_ref[...] = a_im
    a2_im = 2.0 * a_re * a_im
    a2re_ref[...] = a_re * a_re - a_im * a_im
    a2im_ref[...] = a2_im
    na2im_ref[...] = -a2_im
    wb_ref[...] = jnp.zeros(wb_ref.shape, f32)
    cc_ref[...] = jnp.zeros(cc_ref.shape, f32)
    k0_ref[...] = jnp.zeros(k0_ref.shape, f32)
    nt_dims = (((1,), (1,)), ((), ()))
    for g in range(N_GROUPS):
        j, gl = divmod(g, GROUPS_PER_BLOCK)
        c, g8 = divmod(g, GROUPS_PER_CH_BLOCK)
        zr = zr_ref[pl.ds(g, 1), :]
        zi = zi_ref[pl.ds(g, 1), :]
        ar = are_ref[pl.ds(g, 1), :]
        ai = aim_ref[pl.ds(g, 1), :]
        br = brt_ref[g]
        bi = bit_ref[g]
        bbr = zr * br - zi * bi
        bbi = zr * bi + zi * br
        rows = slice(g8 * GROUP_W, (g8 + 1) * GROUP_W)
        rows_odd = slice(CH_BLOCK + g8 * GROUP_W, CH_BLOCK + (g8 + 1) * GROUP_W)
        cols = slice(g8 * N_STATE, (g8 + 1) * N_STATE)
        half_w = GROUPS_PER_CH_BLOCK * N_STATE
        cols_im = slice(half_w + g8 * N_STATE, half_w + (g8 + 1) * N_STATE)
        wb_ref[c, rows, cols] = ar * bbr - ai * bbi
        wb_ref[c, rows, cols_im] = ar * bbi + ai * bbr
        wb_ref[c, rows_odd, cols] = bbr
        wb_ref[c, rows_odd, cols_im] = bbi

        inv = 1.0 / (ar * ar + ai * ai)
        cr = cre_ref[g]
        ci = cim_ref[g]
        cir = (cr * ar + ci * ai) * inv
        cii = (ci * ar - cr * ai) * inv
        k0t = (lax.dot_general(bbr, cir, nt_dims, precision=lax.Precision.HIGHEST, preferred_element_type=f32)
               - lax.dot_general(bbi, cii, nt_dims, precision=lax.Precision.HIGHEST, preferred_element_type=f32))
        grows = slice(gl * GROUP_W, (gl + 1) * GROUP_W)
        k0_ref[j, grows, grows] = k0t

        d_g = jnp.exp(ls_ref[pl.ds(g, 1), :])
        lc = lrec_ref[g]
        mc = limc_ref[g]
        mag_c = jnp.exp(lc * d_g)
        arc = mag_c * jnp.cos(mc * d_g)
        aic = mag_c * jnp.sin(mc * d_g)
        inv_c = 1.0 / (arc * arc + aic * aic)
        crt = crt_ref[g]
        cit = cit_ref[g]
        srows = slice(gl * N_STATE, (gl + 1) * N_STATE)
        srows_im = slice(BLOCK_STATE + gl * N_STATE, BLOCK_STATE + (gl + 1) * N_STATE)
        ecols = slice(BLOCK_IN + gl * GROUP_W, BLOCK_IN + (gl + 1) * GROUP_W)
        cc_ref[j, srows, grows] = crt
        cc_ref[j, srows_im, grows] = -cit
        cc_ref[j, srows, ecols] = (crt * arc + cit * aic) * inv_c
        cc_ref[j, srows_im, ecols] = -((cit * arc - crt * aic) * inv_c)
    wb_out_ref[...] = wb_ref[...].astype(jnp.bfloat16)
    cc_out_ref[...] = cc_ref[...].astype(jnp.bfloat16)
    k0_out_ref[...] = k0_ref[...].astype(jnp.bfloat16)


def _discretise(lam_re, lam_im, log_step, b_re, b_im, c_re, c_im):
    f32, bf16 = jnp.float32, jnp.bfloat16
    brt = jnp.transpose(b_re, (0, 2, 1))
    bit = jnp.transpose(b_im, (0, 2, 1))
    crt = jnp.transpose(c_re, (0, 2, 1))
    cit = jnp.transpose(c_im, (0, 2, 1))
    gp = (N_GROUPS, N_STATE)
    wb_shape = (N_CH_BLOCKS, 2 * CH_BLOCK, 2 * GROUPS_PER_CH_BLOCK * N_STATE)
    cc_shape = (N_BLOCKS, 2 * BLOCK_STATE, 2 * BLOCK_IN)
    k0_shape = (N_BLOCKS, BLOCK_IN, BLOCK_IN)
    out_shape = (
        jax.ShapeDtypeStruct(gp, f32), jax.ShapeDtypeStruct(gp, f32), jax.ShapeDtypeStruct(gp, f32),
        jax.ShapeDtypeStruct(wb_shape, bf16), jax.ShapeDtypeStruct(cc_shape, bf16),
        jax.ShapeDtypeStruct(k0_shape, bf16),
    )
    scratch = [pltpu.VMEM(gp, f32) for _ in range(4)] + [
        pltpu.VMEM(wb_shape, f32), pltpu.VMEM(cc_shape, f32), pltpu.VMEM(k0_shape, f32)]
    return pl.pallas_call(
        _discretise_kernel,
        out_shape=out_shape,
        scratch_shapes=scratch,
        compiler_params=pltpu.CompilerParams(vmem_limit_bytes=VMEM_LIMIT),
        name="discretise",
    )(lam_re, lam_im, log_step.reshape(N_GROUPS, 1), lam_re.reshape(N_GROUPS, N_STATE, 1),
      lam_im.reshape(N_GROUPS, N_STATE, 1), brt, bit, c_re, c_im, crt, cit)


def _mixer_kernel(x_ref, h0_ref, c0_ref, g_ref, win_ref, a1_ref, a2_ref, wb_ref, cc_ref, k0_ref, d_ref,
                  wglu_ref, bglu_ref, wconv_ref, bconv_ref,
                  mix_ref, hout_ref, cout_ref,
                  bu_ref, perm_ref, nat_ref, vbuf_ref, xn_ref, *, tt):
    f32, bf16 = jnp.float32, jnp.bfloat16
    nseq = SEQ_PER_STEP
    m = nseq * tt
    m2 = m // 2
    nth = tt // SUBLANES
    half = nseq * QUAD
    ti = pl.program_id(1)

    @pl.when(ti == 0)
    def _():
        hout_ref[...] = h0_ref[...]
        vbuf_ref[:, 0:SUBLANES, :] = c0_ref[...]

    x = x_ref[...].reshape(m, D_MODEL)
    r = lax.rsqrt(jnp.mean(x * x, axis=-1, keepdims=True) + EPS)
    xn_ref[...] = ((x * r) * g_ref[...]).astype(bf16)

    u = jnp.dot(xn_ref[...], win_ref[:, 0:D_SSM], preferred_element_type=f32)
    for c in range(N_CH_BLOCKS):
        nat_ref[c] = u[:, c * CH_BLOCK:(c + 1) * CH_BLOCK]

    low_rows = lax.broadcasted_iota(jnp.int32, (SUBLANES, LANES), 0) < QUAD

    def halves(first, second, high):
        if high:
            return jnp.where(low_rows, pltpu.roll(first, QUAD, axis=0), second)
        return jnp.where(low_rows, first, pltpu.roll(second, QUAD, axis=0))

    def strided(c, b, th2, parity):
        return pl.ds(b * tt + 2 * SUBLANES * th2 + parity, SUBLANES, stride=2)

    def gather(c, parity):
        out = [None] * (2 * nth)
        for th2 in range(nth // 2):
            for pr in range(nseq // 2):
                e0 = nat_ref[c, strided(c, 2 * pr, th2, parity), :]
                e1 = nat_ref[c, strided(c, 2 * pr + 1, th2, parity), :]
                out[(2 * th2) * 2 + pr] = halves(e0, e1, False)
                out[(2 * th2 + 1) * 2 + pr] = halves(e0, e1, True)
        return jnp.concatenate(out, axis=0)

    sections = 4
    pairs = tt // 2 // sections
    per_gate = D_CONV // GATE_CHUNK
    assert per_gate == N_BLOCKS

    def gate_cols(gate, c):
        start = D_SSM + gate * D_CONV + c * GATE_CHUNK
        return slice(start, start + GATE_CHUNK)

    def project_b(j):
        half_slabs = SLABS // 2
        half_w = GROUPS_PER_CH_BLOCK * N_STATE
        for hb in range(2):
            c = 2 * j + hb
            ccols = slice(c * CH_BLOCK, (c + 1) * CH_BLOCK)
            ev = gather(c, 0)
            od = gather(c, 1)
            perm_ref[0, :, ccols] = ev
            perm_ref[1, :, ccols] = od
            lhs = jnp.concatenate([ev, od], axis=1).astype(bf16)
            v = jnp.dot(lhs, wb_ref[c], preferred_element_type=f32)
            for kk in range(half_slabs):
                k = hb * half_slabs + kk
                re = v[:, kk * LANES:(kk + 1) * LANES]
                im = v[:, half_w + kk * LANES:half_w + (kk + 1) * LANES]
                bu_ref[j % 2, k, :, 0:half, :] = re.reshape(nth, half, LANES)
                bu_ref[j % 2, k, :, half:2 * half, :] = im.reshape(nth, half, LANES)

    def advance(j, q, hs, a1, a2):
        buf = j % 2
        for p in range(q * pairs, (q + 1) * pairs):
            th, p4 = divmod(p, QUAD)
            rows = pl.ds(p4, 2 * nseq, stride=QUAD)
            for k in range(SLABS):
                h = hs[k]
                nh = a1[k] * h + a2[k] * pltpu.roll(h, nseq, axis=0) + bu_ref[buf, k, th, rows, :]
                bu_ref[buf, k, th, rows, :] = nh
                hs[k] = nh
        return hs

    project_b(0)
    for j in range(N_BLOCKS):
        if j + 1 < N_BLOCKS:
            project_b(j + 1)
        a1 = [a1_ref[j * SLABS + k] for k in range(SLABS)]
        a2 = [a2_ref[j * SLABS + k] for k in range(SLABS)]
        hs = [hout_ref[j * SLABS + k] for k in range(SLABS)]

        cols = slice(j * GATE_CHUNK, (j + 1) * GATE_CHUNK)
        gate_b = jnp.dot(xn_ref[...], win_ref[:, gate_cols(0, j)], preferred_element_type=f32)
        hs = advance(j, 0, hs, a1, a2)
        gate_c = jnp.dot(xn_ref[...], win_ref[:, gate_cols(1, j)], preferred_element_type=f32)
        hs = advance(j, 1, hs, a1, a2)
        xv = jnp.dot(xn_ref[...], win_ref[:, gate_cols(2, j)], preferred_element_type=f32)
        hs = advance(j, 2, hs, a1, a2)
        v = gate_c * xv
        vbuf_ref[:, SUBLANES:, cols] = v.reshape(nseq, tt, GATE_CHUNK)
        v1 = vbuf_ref[:, SUBLANES - 1:SUBLANES - 1 + tt, cols].reshape(m, GATE_CHUNK)
        v2 = vbuf_ref[:, SUBLANES - 2:SUBLANES - 2 + tt, cols].reshape(m, GATE_CHUNK)
        yc = (bconv_ref[:, cols] + wconv_ref[0:1, cols] * v2 + wconv_ref[1:2, cols] * v1
              + wconv_ref[2:3, cols] * v)
        mix_ref[:, :, D_SSM + j * GATE_CHUNK:D_SSM + (j + 1) * GATE_CHUNK] = (
            (gate_b * yc).astype(bf16).reshape(nseq, tt, GATE_CHUNK))
        hs = advance(j, 3, hs, a1, a2)

        for k in range(SLABS):
            hout_ref[j * SLABS + k] = hs[k]
        h_all = jnp.concatenate(
            [bu_ref[j % 2, k, :, 0:half, :].reshape(m2, LANES) for k in range(SLABS)]
            + [bu_ref[j % 2, k, :, half:2 * half, :].reshape(m2, LANES) for k in range(SLABS)],
            axis=1).astype(bf16)
        y2 = jnp.dot(h_all, cc_ref[j], preferred_element_type=f32)
        ucols = slice(j * BLOCK_IN, (j + 1) * BLOCK_IN)
        u_even = perm_ref[0, :, ucols]
        u_odd = perm_ref[1, :, ucols]
        feed = jnp.dot(u_odd.astype(bf16), k0_ref[j], preferred_element_type=f32)
        perm_ref[0, :, ucols] = y2[:, BLOCK_IN:2 * BLOCK_IN] - feed + d_ref[:, ucols] * u_even
        perm_ref[1, :, ucols] = y2[:, 0:BLOCK_IN] + d_ref[:, ucols] * u_odd

    tail = vbuf_ref[:, tt:tt + SUBLANES, :]
    vbuf_ref[:, 0:SUBLANES, :] = tail
    cout_ref[...] = tail

    for c in range(N_CH_BLOCKS):
        ccols = slice(c * CH_BLOCK, (c + 1) * CH_BLOCK)
        for parity in range(2):
            for th2 in range(nth // 2):
                for pr in range(nseq // 2):
                    r0 = ((2 * th2) * 2 + pr) * SUBLANES
                    r1 = ((2 * th2 + 1) * 2 + pr) * SUBLANES
                    first = perm_ref[parity, r0:r0 + SUBLANES, ccols]
                    second = perm_ref[parity, r1:r1 + SUBLANES, ccols]
                    nat_ref[c, strided(c, 2 * pr, th2, parity), :] = halves(first, second, False)
                    nat_ref[c, strided(c, 2 * pr + 1, th2, parity), :] = halves(first, second, True)
    y = jnp.concatenate([nat_ref[c] for c in range(N_CH_BLOCKS)], axis=1)
    y = 0.5 * y * (1.0 + lax.erf(y * INV_SQRT2))
    gl = jnp.dot(y.astype(bf16), wglu_ref[...], preferred_element_type=f32) + bglu_ref[...]
    y_ssm = y * jax.nn.sigmoid(gl)
    mix_ref[:, :, 0:D_SSM] = y_ssm.astype(bf16).reshape(nseq, tt, D_SSM)


def _const_spec(shape):
    zeros = (0,) * len(shape)
    return pl.BlockSpec(shape, lambda *_: zeros, pipeline_mode=pl.Buffered(1))


def _mixer(x, h0, c0, norm_g, w_in, a1_tab, a2_tab, wb, cc, k0, d, w_glu, b_glu, w_conv, b_conv, *, tt):
    nbg, nseq, seqlen, _ = x.shape
    assert nseq == SEQ_PER_STEP and seqlen % tt == 0 and tt % (8 * SUBLANES) == 0
    nt = seqlen // tt
    m = nseq * tt
    f32 = jnp.float32
    nstate_rows = N_BLOCKS * SLABS
    in_specs = [
        pl.BlockSpec((None, nseq, tt, D_MODEL), lambda bg, ti: (bg, 0, ti, 0)),
        pl.BlockSpec((None, nstate_rows, SUBLANES, LANES), lambda bg, ti: (bg, 0, 0, 0)),
        pl.BlockSpec((None, nseq, SUBLANES, D_CONV), lambda bg, ti: (bg, 0, 0, 0)),
        _const_spec((1, D_MODEL)),
        _const_spec((D_MODEL, D_IN)),
        _const_spec((nstate_rows, SUBLANES, LANES)),
        _const_spec((nstate_rows, SUBLANES, LANES)),
        _const_spec((N_CH_BLOCKS, 2 * CH_BLOCK, 2 * GROUPS_PER_CH_BLOCK * N_STATE)),
        _const_spec((N_BLOCKS, 2 * BLOCK_STATE, 2 * BLOCK_IN)),
        _const_spec((N_BLOCKS, BLOCK_IN, BLOCK_IN)),
        _const_spec((1, D_SSM)),
        _const_spec((D_SSM, D_SSM)),
        _const_spec((1, D_SSM)),
        _const_spec((CONV_WIDTH, D_CONV)),
        _const_spec((1, D_CONV)),
    ]
    out_specs = [
        pl.BlockSpec((None, nseq, tt, D_MODEL), lambda bg, ti: (bg, 0, ti, 0)),
        pl.BlockSpec((None, nstate_rows, SUBLANES, LANES), lambda bg, ti: (bg, 0, 0, 0)),
        pl.BlockSpec((None, nseq, SUBLANES, D_CONV), lambda bg, ti: (bg, 0, 0, 0)),
    ]
    out_shape = (
        jax.ShapeDtypeStruct((nbg, nseq, seqlen, D_MODEL), jnp.bfloat16),
        jax.ShapeDtypeStruct((nbg, nstate_rows, SUBLANES, LANES), f32),
        jax.ShapeDtypeStruct((nbg, nseq, SUBLANES, D_CONV), f32),
    )
    scratch = [
        pltpu.VMEM((2, SLABS, tt // SUBLANES, QUAD_ROWS, LANES), f32),
        pltpu.VMEM((2, m // 2, D_SSM), f32),
        pltpu.VMEM((N_CH_BLOCKS, m, CH_BLOCK), f32),
        pltpu.VMEM((nseq, tt + SUBLANES, D_CONV), f32),
        pltpu.VMEM((m, D_MODEL), jnp.bfloat16),
    ]
    return pl.pallas_call(
        functools.partial(_mixer_kernel, tt=tt),
        grid=(nbg, nt),
        in_specs=in_specs,
        out_specs=out_specs,
        out_shape=out_shape,
        scratch_shapes=scratch,
        compiler_params=pltpu.CompilerParams(
            dimension_semantics=("arbitrary", "arbitrary"), vmem_limit_bytes=VMEM_LIMIT),
        name="mixer",
    )(x, h0, c0, norm_g, w_in, a1_tab, a2_tab, wb, cc, k0, d, w_glu, b_glu, w_conv, b_conv)


def _mlp_kernel(x_ref, mix_ref, wout_ref, g_ref, wup_ref, wdown_ref, gfin_ref, o_ref, xn_ref):
    f32, bf16 = jnp.float32, jnp.bfloat16
    f = pl.program_id(1)

    @pl.when(f == 0)
    def _():
        x1 = x_ref[...] + jnp.dot(mix_ref[...], wout_ref[...], preferred_element_type=f32)
        o_ref[...] = x1
        r = lax.rsqrt(jnp.mean(x1 * x1, axis=-1, keepdims=True) + EPS)
        xn_ref[...] = ((x1 * r) * g_ref[...]).astype(bf16)

    hm = jnp.dot(xn_ref[...], wup_ref[...], preferred_element_type=f32)
    act = jnp.square(jnp.maximum(hm, 0.0)).astype(bf16)
    o_ref[...] += jnp.dot(act, wdown_ref[...], preferred_element_type=f32)

    @pl.when(f == pl.num_programs(1) - 1)
    def _():
        x2 = o_ref[...]
        r = lax.rsqrt(jnp.mean(x2 * x2, axis=-1, keepdims=True) + EPS)
        o_ref[...] = (x2 * r) * gfin_ref[...]


def _mlp(x, mix, w_out, norm_g, w_up, w_down, norm_final_g, *, tm):
    tokens = x.shape[0]
    tf = w_up.shape[2]
    assert tokens % tm == 0 and w_up.shape == (D_FF // tf, D_MODEL, tf)
    in_specs = [
        pl.BlockSpec((tm, D_MODEL), lambda i, f: (i, 0)),
        pl.BlockSpec((tm, D_MODEL), lambda i, f: (i, 0)),
        _const_spec((D_MODEL, D_MODEL)),
        _const_spec((1, D_MODEL)),
        pl.BlockSpec((None, D_MODEL, tf), lambda i, f: (f, 0, 0)),
        pl.BlockSpec((tf, D_MODEL), lambda i, f: (f, 0)),
        _const_spec((1, D_MODEL)),
    ]
    return pl.pallas_call(
        _mlp_kernel,
        grid=(tokens // tm, D_FF // tf),
        in_specs=in_specs,
        out_specs=pl.BlockSpec((tm, D_MODEL), lambda i, f: (i, 0)),
        out_shape=jax.ShapeDtypeStruct((tokens, D_MODEL), jnp.float32),
        scratch_shapes=[pltpu.VMEM((tm, D_MODEL), jnp.bfloat16)],
        compiler_params=pltpu.CompilerParams(
            dimension_semantics=("arbitrary", "arbitrary"), vmem_limit_bytes=VMEM_LIMIT),
        name="mlp",
    )(x, mix, w_out, norm_g, w_up, w_down, norm_final_g)


def _mlp_cast_kernel(x_ref, mix_ref, wout_ref, g_ref, wup_ref, wdown_ref, gfin_ref,
                     o_ref, wup_bf_ref, wdown_bf_ref, xn_ref):
    f32, bf16 = jnp.float32, jnp.bfloat16
    f = pl.program_id(0)
    wup = wup_ref[...].astype(bf16)
    wdown = wdown_ref[...].astype(bf16)
    wup_bf_ref[...] = wup
    wdown_bf_ref[...] = wdown

    @pl.when(f == 0)
    def _():
        x1 = x_ref[...] + jnp.dot(mix_ref[...], wout_ref[...], preferred_element_type=f32)
        o_ref[...] = x1
        r = lax.rsqrt(jnp.mean(x1 * x1, axis=-1, keepdims=True) + EPS)
        xn_ref[...] = ((x1 * r) * g_ref[...]).astype(bf16)

    hm = jnp.dot(xn_ref[...], wup, preferred_element_type=f32)
    act = jnp.square(jnp.maximum(hm, 0.0)).astype(bf16)
    o_ref[...] += jnp.dot(act, wdown, preferred_element_type=f32)

    @pl.when(f == pl.num_programs(0) - 1)
    def _():
        x2 = o_ref[...]
        r = lax.rsqrt(jnp.mean(x2 * x2, axis=-1, keepdims=True) + EPS)
        o_ref[...] = (x2 * r) * gfin_ref[...]


def _mlp_cast(x, mix, w_out, norm_g, w_up_f32, w_down_f32, norm_final_g, *, tf, tf_out):
    tokens = x.shape[0]
    assert D_FF % tf_out == 0 and tf_out % tf == 0
    per_tile = tf_out // tf
    bf16 = jnp.bfloat16
    in_specs = [
        _const_spec((tokens, D_MODEL)),
        _const_spec((tokens, D_MODEL)),
        _const_spec((D_MODEL, D_MODEL)),
        _const_spec((1, D_MODEL)),
        pl.BlockSpec((D_MODEL, tf), lambda f: (0, f)),
        pl.BlockSpec((tf, D_MODEL), lambda f: (f, 0)),
        _const_spec((1, D_MODEL)),
    ]
    out_specs = [
        pl.BlockSpec((tokens, D_MODEL), lambda f: (0, 0)),
        pl.BlockSpec((None, D_MODEL, tf), lambda f: (f // per_tile, 0, f % per_tile)),
        pl.BlockSpec((tf, D_MODEL), lambda f: (f, 0)),
    ]
    out_shape = (
        jax.ShapeDtypeStruct((tokens, D_MODEL), jnp.float32),
        jax.ShapeDtypeStruct((D_FF // tf_out, D_MODEL, tf_out), bf16),
        jax.ShapeDtypeStruct((D_FF, D_MODEL), bf16),
    )
    return pl.pallas_call(
        _mlp_cast_kernel,
        grid=(D_FF // tf,),
        in_specs=in_specs,
        out_specs=out_specs,
        out_shape=out_shape,
        scratch_shapes=[pltpu.VMEM((tokens, D_MODEL), bf16)],
        compiler_params=pltpu.CompilerParams(
            dimension_semantics=("arbitrary",), vmem_limit_bytes=VMEM_LIMIT),
        name="mlp_cast",
    )(x, mix, w_out, norm_g, w_up_f32, w_down_f32, norm_final_g)


def _state_to_rows(h_re, h_im):
    def one(h):
        nb = h.shape[0]
        h = h.reshape(nb // SEQ_PER_STEP, SEQ_PER_STEP, N_BLOCKS, SLABS, LANES)
        return jnp.transpose(h, (0, 2, 3, 1, 4))
    both = jnp.concatenate([one(h_re), one(h_im)], axis=3)
    return both.reshape(both.shape[0], N_BLOCKS * SLABS, 2 * SEQ_PER_STEP, LANES)


def _rows_to_state(rows):
    nbg = rows.shape[0]
    r = rows.reshape(nbg, N_BLOCKS, SLABS, 2, SEQ_PER_STEP, LANES)
    r = jnp.transpose(r, (3, 0, 4, 1, 2, 5))
    r = r.reshape(2, nbg * SEQ_PER_STEP, N_GROUPS, N_STATE)
    return r[0], r[1]


def _a_tables(a_re, a_im, neg_a_im):
    def rows(t, n):
        return jnp.broadcast_to(t.reshape(N_BLOCKS * SLABS, 1, LANES), (N_BLOCKS * SLABS, n, LANES))
    a1 = rows(a_re, 2 * SEQ_PER_STEP)
    a2 = jnp.concatenate([rows(neg_a_im, SEQ_PER_STEP), rows(a_im, SEQ_PER_STEP)], axis=1)
    return a1, a2


def _run_mixer(x, h_re, h_im, conv_prev, lw, tabs, *, tt):
    bsz, seqlen, _ = x.shape
    nbg = bsz // SEQ_PER_STEP
    x4 = x.reshape(nbg, SEQ_PER_STEP, seqlen, D_MODEL)
    h0 = _state_to_rows(h_re, h_im)
    c0 = jnp.pad(conv_prev, ((0, 0), (SUBLANES - (CONV_WIDTH - 1), 0), (0, 0)))
    c0 = c0.reshape(nbg, SEQ_PER_STEP, SUBLANES, D_CONV)
    mix, hout, cout = _mixer(x4, h0, c0, lw["norm_mix_g"], lw["w_in"], *tabs, lw["d"],
                             lw["w_glu"], lw["b_glu"], lw["w_conv"], lw["b_conv"], tt=tt)
    new_re, new_im = _rows_to_state(hout)
    new_conv = cout.reshape(bsz, SUBLANES, D_CONV)[:, SUBLANES - (CONV_WIDTH - 1):, :]
    return mix.reshape(bsz * seqlen, D_MODEL), new_re, new_im, new_conv


def kernel(x_prompt, x_sample, state_ssm_re, state_ssm_im, cache_conv, norm_mix_g, w_in, ssm_lambda_re, ssm_lambda_im, ssm_log_step, ssm_b_re, ssm_b_im, ssm_c_re, ssm_c_im, ssm_d, w_glu, b_glu, w_conv, b_conv, w_out, norm_mlp_g, w_up, w_down, norm_final_g):
    depth = w_in.shape[0]
    assert depth == 1, "the final norm is fused into the last (only) layer"
    bf16 = jnp.bfloat16
    bp = x_prompt.shape[0]
    a_re, a_im, neg_a_im, wb, cc, k0 = _discretise(
        ssm_lambda_re[0], ssm_lambda_im[0], ssm_log_step[0], ssm_b_re[0], ssm_b_im[0], ssm_c_re[0], ssm_c_im[0])
    tabs = _a_tables(a_re, a_im, neg_a_im) + (wb, cc, k0)
    lw = dict(
        norm_mix_g=norm_mix_g[0].reshape(1, D_MODEL), w_in=w_in[0].astype(bf16),
        d=ssm_d[0].reshape(1, D_SSM), w_glu=w_glu[0].astype(bf16), b_glu=b_glu[0].reshape(1, D_SSM),
        w_conv=w_conv[0], b_conv=b_conv[0].reshape(1, D_CONV), w_out=w_out[0].astype(bf16),
        norm_mlp_g=norm_mlp_g[0].reshape(1, D_MODEL))
    gfin = norm_final_g.reshape(1, D_MODEL)

    zeros_h = jnp.zeros((bp, N_GROUPS, N_STATE), jnp.float32)
    zeros_c = jnp.zeros((bp, CONV_WIDTH - 1, D_CONV), x_prompt.dtype)
    mix_s, sre, sim, scv = _run_mixer(x_sample, state_ssm_re[0], state_ssm_im[0], cache_conv[0], lw, tabs, tt=64)
    mix_p, pre, pim, pcv = _run_mixer(x_prompt, zeros_h, zeros_h, zeros_c, lw, tabs, tt=128)
    ys, w_up_bf, w_down_bf = _mlp_cast(x_sample.reshape(-1, D_MODEL), mix_s, lw["w_out"], lw["norm_mlp_g"],
                                       w_up.reshape(D_MODEL, D_FF), w_down.reshape(D_FF, D_MODEL), gfin, tf=512, tf_out=1024)
    yp = _mlp(x_prompt.reshape(-1, D_MODEL), mix_p, lw["w_out"], lw["norm_mlp_g"], w_up_bf, w_down_bf, gfin,
              tm=512)
    return (yp.reshape(x_prompt.shape), ys.reshape(x_sample.shape), pre[None], pim[None], pcv[None],
            sre[None], sim[None], scv[None])
```

```python
import functools

import jax
import jax.numpy as jnp
from jax import lax
from jax.experimental import pallas as pl
from jax.experimental.pallas import tpu as pltpu

D_MODEL = 2048
D_SSM = 1024
D_CONV = 1024
N_GROUPS = 64
GROUP_W = 16
N_STATE = 64
D_FF = 8192
D_IN = D_SSM + 3 * D_CONV
CONV_WIDTH = 3
EPS = 1e-6
INV_SQRT2 = 0.7071067811865476

SUBLANES = 8
LANES = 128
SEQ_PER_STEP = 4
GROUPS_PER_BLOCK = 16
N_BLOCKS = N_GROUPS // GROUPS_PER_BLOCK
BLOCK_IN = GROUPS_PER_BLOCK * GROUP_W
BLOCK_STATE = GROUPS_PER_BLOCK * N_STATE
SLABS = BLOCK_STATE // LANES
GATE_CHUNK = 256
QUAD = 4
QUAD_ROWS = 2 * SEQ_PER_STEP * QUAD
CH_BLOCK = LANES
N_CH_BLOCKS = D_SSM // CH_BLOCK
GROUPS_PER_CH_BLOCK = CH_BLOCK // GROUP_W
VMEM_LIMIT = 60 * 1024 * 1024
PROMPT_TILE_T = 128
SAMPLE_TILE_T = 64
MLP_TILE_M = 512
MLP_TILE_F = 1024
CAST_TILE_F = 512


def _discretise_kernel(lre_ref, lim_ref, ls_ref, lret_ref, limt_ref, lst_ref, brt_ref, bit_ref, cre_ref, cim_ref,
                       crt_ref, cit_ref,
                       a2re_ref, a2im_ref, na2im_ref, wb_out_ref, cc_out_ref, k0_out_ref,
                       zr_ref, zi_ref, are_ref, aim_ref, icr_ref, ici_ref, wb_ref, cc_ref, k0_ref):
    f32 = jnp.float32
    lre = lre_ref[...]
    lim = lim_ref[...]
    delta = jnp.exp(ls_ref[...])
    mag = jnp.exp(lre * delta)
    a_re = mag * jnp.cos(lim * delta)
    a_im = mag * jnp.sin(lim * delta)
    den = lre * lre + lim * lim
    zr_ref[...] = ((a_re - 1.0) * lre + a_im * lim) / den
    zi_ref[...] = (a_im * lre - (a_re - 1.0) * lim) / den
    are_ref[...] = a_re
    aim_ref[...] = a_im
    a2_im = 2.0 * a_re * a_im
    a2re_ref[...] = a_re * a_re - a_im * a_im
    a2im_ref[...] = a2_im
    na2im_ref[...] = -a2_im
    delta_t = jnp.exp(lst_ref[...])
    mag_t = jnp.exp(lret_ref[...] * delta_t)
    art = mag_t * jnp.cos(limt_ref[...] * delta_t)
    ait = mag_t * jnp.sin(limt_ref[...] * delta_t)
    inv_t = 1.0 / (art * art + ait * ait)
    icr_ref[...] = art * inv_t
    ici_ref[...] = -(ait * inv_t)
    wb_ref[...] = jnp.zeros(wb_ref.shape, f32)
    cc_ref[...] = jnp.zeros(cc_ref.shape, f32)
    k0_ref[...] = jnp.zeros(k0_ref.shape, f32)
    nt_dims = (((1,), (1,)), ((), ()))
    for g in range(N_GROUPS):
        j, gl = divmod(g, GROUPS_PER_BLOCK)
        c, g8 = divmod(g, GROUPS_PER_CH_BLOCK)
        zr = zr_ref[pl.ds(g, 1), :]
        zi = zi_ref[pl.ds(g, 1), :]
        ar = are_ref[pl.ds(g, 1), :]
        ai = aim_ref[pl.ds(g, 1), :]
        br = brt_ref[g]
        bi = bit_ref[g]
        bbr = zr * br - zi * bi
        bbi = zr * bi + zi * br
        rows = slice(g8 * GROUP_W, (g8 + 1) * GROUP_W)
        rows_odd = slice(CH_BLOCK + g8 * GROUP_W, CH_BLOCK + (g8 + 1) * GROUP_W)
        cols = slice(g8 * N_STATE, (g8 + 1) * N_STATE)
        half_w = GROUPS_PER_CH_BLOCK * N_STATE
        cols_im = slice(half_w + g8 * N_STATE, half_w + (g8 + 1) * N_STATE)
        wb_ref[c, rows, cols] = ar * bbr - ai * bbi
        wb_ref[c, rows, cols_im] = ar * bbi + ai * bbr
        wb_ref[c, rows_odd, cols] = bbr
        wb_ref[c, rows_odd, cols_im] = bbi

        inv = 1.0 / (ar * ar + ai * ai)
        cr = cre_ref[g]
        ci = cim_ref[g]
        cir = (cr * ar + ci * ai) * inv
        cii = (ci * ar - cr * ai) * inv
        k0t = (lax.dot_general(bbr, cir, nt_dims, precision=lax.Precision.HIGHEST, preferred_element_type=f32)
               - lax.dot_general(bbi, cii, nt_dims, precision=lax.Precision.HIGHEST, preferred_element_type=f32))
        grows = slice(gl * GROUP_W, (gl + 1) * GROUP_W)
        k0_ref[j, grows, grows] = k0t

        icr = icr_ref[:, g:g + 1]
        ici = ici_ref[:, g:g + 1]
        crt = crt_ref[g]
        cit = cit_ref[g]
        srows = slice(gl * N_STATE, (gl + 1) * N_STATE)
        srows_im = slice(BLOCK_STATE + gl * N_STATE, BLOCK_STATE + (gl + 1) * N_STATE)
        ecols = slice(BLOCK_IN + gl * GROUP_W, BLOCK_IN + (gl + 1) * GROUP_W)
        cc_ref[j, srows, grows] = crt
        cc_ref[j, srows_im, grows] = -cit
        cc_ref[j, srows, ecols] = crt * icr - cit * ici
        cc_ref[j, srows_im, ecols] = -(cit * icr + crt * ici)
    wb_out_ref[...] = wb_ref[...].astype(jnp.bfloat16)
    cc_out_ref[...] = cc_ref[...].astype(jnp.bfloat16)
    k0_out_ref[...] = k0_ref[...].astype(jnp.bfloat16)


def _discretise(lam_re, lam_im, log_step, b_re, b_im, c_re, c_im):
    f32, bf16 = jnp.float32, jnp.bfloat16
    brt = jnp.transpose(b_re, (0, 2, 1))
    bit = jnp.transpose(b_im, (0, 2, 1))
    crt = jnp.transpose(c_re, (0, 2, 1))
    cit = jnp.transpose(c_im, (0, 2, 1))
    gp = (N_GROUPS, N_STATE)
    wb_shape = (N_CH_BLOCKS, 2 * CH_BLOCK, 2 * GROUPS_PER_CH_BLOCK * N_STATE)
    cc_shape = (N_BLOCKS, 2 * BLOCK_STATE, 2 * BLOCK_IN)
    k0_shape = (N_BLOCKS, BLOCK_IN, BLOCK_IN)
    out_shape = (
        jax.ShapeDtypeStruct(gp, f32), jax.ShapeDtypeStruct(gp, f32), jax.ShapeDtypeStruct(gp, f32),
        jax.ShapeDtypeStruct(wb_shape, bf16), jax.ShapeDtypeStruct(cc_shape, bf16),
        jax.ShapeDtypeStruct(k0_shape, bf16),
    )
    scratch = [pltpu.VMEM(gp, f32) for _ in range(4)] + [pltpu.VMEM((N_STATE, N_GROUPS), f32) for _ in range(2)] + [
        pltpu.VMEM(wb_shape, f32), pltpu.VMEM(cc_shape, f32), pltpu.VMEM(k0_shape, f32)]
    return pl.pallas_call(
        _discretise_kernel,
        out_shape=out_shape,
        scratch_shapes=scratch,
        compiler_params=pltpu.CompilerParams(vmem_limit_bytes=VMEM_LIMIT),
        name="discretise",
    )(lam_re, lam_im, log_step.reshape(N_GROUPS, 1), lam_re.T, lam_im.T, log_step.reshape(1, N_GROUPS),
      brt, bit, c_re, c_im, crt, cit)


def _mixer_kernel(x_ref, h0_ref, c0_ref, g_ref, win_ref, a1_ref, a2_ref, wb_ref, cc_ref, k0_ref, d_ref,
                  wglu_ref, bglu_ref, wconv_ref, bconv_ref,
                  mix_ref, hout_ref, cout_ref,
                  bu_ref, perm_ref, nat_ref, vbuf_ref, xn_ref, *, tt):
    f32, bf16 = jnp.float32, jnp.bfloat16
    nseq = SEQ_PER_STEP
    m = nseq * tt
    m2 = m // 2
    nth = tt // SUBLANES
    half = nseq * QUAD
    ti = pl.program_id(1)

    @pl.when(ti == 0)
    def _():
        hout_ref[...] = h0_ref[...]
        vbuf_ref[:, 0:SUBLANES, :] = c0_ref[...]

    for hb in range(2):
        rows = slice(hb * m2, (hb + 1) * m2)
        x = x_ref[2 * hb:2 * hb + 2].reshape(m2, D_MODEL)
        r = lax.rsqrt(jnp.mean(x * x, axis=-1, keepdims=True) + EPS)
        xn_ref[rows, :] = ((x * r) * g_ref[...]).astype(bf16)
        u = jnp.dot(xn_ref[rows, :], win_ref[:, 0:D_SSM], preferred_element_type=f32)
        for c in range(N_CH_BLOCKS):
            nat_ref[c, rows, :] = u[:, c * CH_BLOCK:(c + 1) * CH_BLOCK]

    low_rows = lax.broadcasted_iota(jnp.int32, (SUBLANES, LANES), 0) < QUAD

    def halves(first, second, high):
        if high:
            return jnp.where(low_rows, pltpu.roll(first, QUAD, axis=0), second)
        return jnp.where(low_rows, first, pltpu.roll(second, QUAD, axis=0))

    def strided(c, b, th2, parity):
        return pl.ds(b * tt + 2 * SUBLANES * th2 + parity, SUBLANES, stride=2)

    def gather(c, parity):
        out = [None] * (2 * nth)
        for th2 in range(nth // 2):
            for pr in range(nseq // 2):
                e0 = nat_ref[c, strided(c, 2 * pr, th2, parity), :]
                e1 = nat_ref[c, strided(c, 2 * pr + 1, th2, parity), :]
                out[(2 * th2) * 2 + pr] = halves(e0, e1, False)
                out[(2 * th2 + 1) * 2 + pr] = halves(e0, e1, True)
        return jnp.concatenate(out, axis=0)

    sections = 4
    pairs = tt // 2 // sections
    per_gate = D_CONV // GATE_CHUNK
    assert per_gate == N_BLOCKS

    def gate_cols(gate, c):
        start = D_SSM + gate * D_CONV + c * GATE_CHUNK
        return slice(start, start + GATE_CHUNK)

    def project_b(j):
        half_slabs = SLABS // 2
        half_w = GROUPS_PER_CH_BLOCK * N_STATE
        for hb in range(2):
            c = 2 * j + hb
            ccols = slice(c * CH_BLOCK, (c + 1) * CH_BLOCK)
            ev = gather(c, 0)
            od = gather(c, 1)
            perm_ref[0, :, ccols] = ev
            perm_ref[1, :, ccols] = od
            lhs = jnp.concatenate([ev, od], axis=1).astype(bf16)
            v = jnp.dot(lhs, wb_ref[c], preferred_element_type=f32)
            for kk in range(half_slabs):
                k = hb * half_slabs + kk
                re = v[:, kk * LANES:(kk + 1) * LANES]
                im = v[:, half_w + kk * LANES:half_w + (kk + 1) * LANES]
                bu_ref[j % 2, k, :, 0:half, :] = re.reshape(nth, half, LANES)
                bu_ref[j % 2, k, :, half:2 * half, :] = im.reshape(nth, half, LANES)

    def advance(j, q, hs, a1, a2):
        buf = j % 2
        for p in range(q * pairs, (q + 1) * pairs):
            th, p4 = divmod(p, QUAD)
            rows = pl.ds(p4, 2 * nseq, stride=QUAD)
            for k in range(SLABS):
                h = hs[k]
                nh = a1[k] * h + a2[k] * pltpu.roll(h, nseq, axis=0) + bu_ref[buf, k, th, rows, :]
                bu_ref[buf, k, th, rows, :] = nh
                hs[k] = nh
        return hs

    project_b(0)
    for j in range(N_BLOCKS):
        if j + 1 < N_BLOCKS:
            project_b(j + 1)
        a1 = [a1_ref[j * SLABS + k] for k in range(SLABS)]
        a2 = [a2_ref[j * SLABS + k] for k in range(SLABS)]
        hs = [hout_ref[j * SLABS + k] for k in range(SLABS)]

        cols = slice(j * GATE_CHUNK, (j + 1) * GATE_CHUNK)
        gate_b = jnp.dot(xn_ref[...], win_ref[:, gate_cols(0, j)], preferred_element_type=f32)
        hs = advance(j, 0, hs, a1, a2)
        gate_c = jnp.dot(xn_ref[...], win_ref[:, gate_cols(1, j)], preferred_element_type=f32)
        hs = advance(j, 1, hs, a1, a2)
        xv = jnp.dot(xn_ref[...], win_ref[:, gate_cols(2, j)], preferred_element_type=f32)
        hs = advance(j, 2, hs, a1, a2)
        v = gate_c * xv
        vbuf_ref[:, SUBLANES:, cols] = v.reshape(nseq, tt, GATE_CHUNK)
        v1 = vbuf_ref[:, SUBLANES - 1:SUBLANES - 1 + tt, cols].reshape(m, GATE_CHUNK)
        v2 = vbuf_ref[:, SUBLANES - 2:SUBLANES - 2 + tt, cols].reshape(m, GATE_CHUNK)
        yc = (bconv_ref[:, cols] + wconv_ref[0:1, cols] * v2 + wconv_ref[1:2, cols] * v1
              + wconv_ref[2:3, cols] * v)
        mix_ref[:, :, D_SSM + j * GATE_CHUNK:D_SSM + (j + 1) * GATE_CHUNK] = (
            (gate_b * yc).astype(bf16).reshape(nseq, tt, GATE_CHUNK))
        hs = advance(j, 3, hs, a1, a2)

        for k in range(SLABS):
            hout_ref[j * SLABS + k] = hs[k]
        h_all = jnp.concatenate(
            [bu_ref[j % 2, k, :, 0:half, :].reshape(m2, LANES) for k in range(SLABS)]
            + [bu_ref[j % 2, k, :, half:2 * half, :].reshape(m2, LANES) for k in range(SLABS)],
            axis=1).astype(bf16)
        y2 = jnp.dot(h_all, cc_ref[j], preferred_element_type=f32)
        ucols = slice(j * BLOCK_IN, (j + 1) * BLOCK_IN)
        u_even = perm_ref[0, :, ucols]
        u_odd = perm_ref[1, :, ucols]
        feed = jnp.dot(u_odd.astype(bf16), k0_ref[j], preferred_element_type=f32)
        perm_ref[0, :, ucols] = y2[:, BLOCK_IN:2 * BLOCK_IN] - feed + d_ref[:, ucols] * u_even
        perm_ref[1, :, ucols] = y2[:, 0:BLOCK_IN] + d_ref[:, ucols] * u_odd

    tail = vbuf_ref[:, tt:tt + SUBLANES, :]
    vbuf_ref[:, 0:SUBLANES, :] = tail
    cout_ref[...] = tail

    for c in range(N_CH_BLOCKS):
        ccols = slice(c * CH_BLOCK, (c + 1) * CH_BLOCK)
        for parity in range(2):
            for th2 in range(nth // 2):
                for pr in range(nseq // 2):
                    r0 = ((2 * th2) * 2 + pr) * SUBLANES
                    r1 = ((2 * th2 + 1) * 2 + pr) * SUBLANES
                    first = perm_ref[parity, r0:r0 + SUBLANES, ccols]
                    second = perm_ref[parity, r1:r1 + SUBLANES, ccols]
                    nat_ref[c, strided(c, 2 * pr, th2, parity), :] = halves(first, second, False)
                    nat_ref[c, strided(c, 2 * pr + 1, th2, parity), :] = halves(first, second, True)
    for hb in range(2):
        rows = slice(hb * m2, (hb + 1) * m2)
        y = jnp.concatenate([nat_ref[c, rows, :] for c in range(N_CH_BLOCKS)], axis=1)
        y = 0.5 * y * (1.0 + lax.erf(y * INV_SQRT2))
        gl = jnp.dot(y.astype(bf16), wglu_ref[...], preferred_element_type=f32) + bglu_ref[...]
        y_ssm = y * jax.nn.sigmoid(gl)
        mix_ref[2 * hb:2 * hb + 2, :, 0:D_SSM] = y_ssm.astype(bf16).reshape(nseq // 2, tt, D_SSM)


def _const_spec(shape):
    zeros = (0,) * len(shape)
    return pl.BlockSpec(shape, lambda *_: zeros, pipeline_mode=pl.Buffered(1))


def _mixer(x, h0, c0, norm_g, w_in, a1_tab, a2_tab, wb, cc, k0, d, w_glu, b_glu, w_conv, b_conv, *, tt):
    nbg, nseq, seqlen, _ = x.shape
    assert nseq == SEQ_PER_STEP and seqlen % tt == 0 and tt % (8 * SUBLANES) == 0
    nt = seqlen // tt
    m = nseq * tt
    f32 = jnp.float32
    nstate_rows = N_BLOCKS * SLABS
    in_specs = [
        pl.BlockSpec((None, nseq, tt, D_MODEL), lambda bg, ti: (bg, 0, ti, 0)),
        pl.BlockSpec((None, nstate_rows, SUBLANES, LANES), lambda bg, ti: (bg, 0, 0, 0)),
        pl.BlockSpec((None, nseq, SUBLANES, D_CONV), lambda bg, ti: (bg, 0, 0, 0)),
        _const_spec((1, D_MODEL)),
        _const_spec((D_MODEL, D_IN)),
        _const_spec((nstate_rows, SUBLANES, LANES)),
        _const_spec((nstate_rows, SUBLANES, LANES)),
        _const_spec((N_CH_BLOCKS, 2 * CH_BLOCK, 2 * GROUPS_PER_CH_BLOCK * N_STATE)),
        _const_spec((N_BLOCKS, 2 * BLOCK_STATE, 2 * BLOCK_IN)),
        _const_spec((N_BLOCKS, BLOCK_IN, BLOCK_IN)),
        _const_spec((1, D_SSM)),
        _const_spec((D_SSM, D_SSM)),
        _const_spec((1, D_SSM)),
        _const_spec((CONV_WIDTH, D_CONV)),
        _const_spec((1, D_CONV)),
    ]
    out_specs = [
        pl.BlockSpec((None, nseq, tt, D_MODEL), lambda bg, ti: (bg, 0, ti, 0)),
        pl.BlockSpec((None, nstate_rows, SUBLANES, LANES), lambda bg, ti: (bg, 0, 0, 0)),
        pl.BlockSpec((None, nseq, SUBLANES, D_CONV), lambda bg, ti: (bg, 0, 0, 0)),
    ]
    out_shape = (
        jax.ShapeDtypeStruct((nbg, nseq, seqlen, D_MODEL), jnp.bfloat16),
        jax.ShapeDtypeStruct((nbg, nstate_rows, SUBLANES, LANES), f32),
        jax.ShapeDtypeStruct((nbg, nseq, SUBLANES, D_CONV), f32),
    )
    scratch = [
        pltpu.VMEM((2, SLABS, tt // SUBLANES, QUAD_ROWS, LANES), f32),
        pltpu.VMEM((2, m // 2, D_SSM), f32),
        pltpu.VMEM((N_CH_BLOCKS, m, CH_BLOCK), f32),
        pltpu.VMEM((nseq, tt + SUBLANES, D_CONV), f32),
        pltpu.VMEM((m, D_MODEL), jnp.bfloat16),
    ]
    return pl.pallas_call(
        functools.partial(_mixer_kernel, tt=tt),
        grid=(nbg, nt),
        in_specs=in_specs,
        out_specs=out_specs,
        out_shape=out_shape,
        scratch_shapes=scratch,
        compiler_params=pltpu.CompilerParams(
            dimension_semantics=("arbitrary", "arbitrary"), vmem_limit_bytes=VMEM_LIMIT),
        name="mixer",
    )(x, h0, c0, norm_g, w_in, a1_tab, a2_tab, wb, cc, k0, d, w_glu, b_glu, w_conv, b_conv)


def _mlp_kernel(x_ref, mix_ref, wout_ref, g_ref, wup_ref, wdown_ref, gfin_ref, o_ref, xn_ref):
    f32, bf16 = jnp.float32, jnp.bfloat16
    f = pl.program_id(1)

    @pl.when(f == 0)
    def _():
        x1 = x_ref[...] + jnp.dot(mix_ref[...], wout_ref[...], preferred_element_type=f32)
        o_ref[...] = x1
        r = lax.rsqrt(jnp.mean(x1 * x1, axis=-1, keepdims=True) + EPS)
        xn_ref[...] = ((x1 * r) * g_ref[...]).astype(bf16)

    hm = jnp.dot(xn_ref[...], wup_ref[...], preferred_element_type=f32)
    act = jnp.square(jnp.maximum(hm, 0.0)).astype(bf16)
    o_ref[...] += jnp.dot(act, wdown_ref[...], preferred_element_type=f32)

    @pl.when(f == pl.num_programs(1) - 1)
    def _():
        x2 = o_ref[...]
        r = lax.rsqrt(jnp.mean(x2 * x2, axis=-1, keepdims=True) + EPS)
        o_ref[...] = (x2 * r) * gfin_ref[...]


def _mlp(x, mix, w_out, norm_g, w_up, w_down, norm_final_g, *, tm):
    tokens = x.shape[0]
    tf = w_up.shape[2]
    assert tokens % tm == 0 and w_up.shape == (D_FF // tf, D_MODEL, tf)
    in_specs = [
        pl.BlockSpec((tm, D_MODEL), lambda i, f: (i, 0)),
        pl.BlockSpec((tm, D_MODEL), lambda i, f: (i, 0)),
        _const_spec((D_MODEL, D_MODEL)),
        _const_spec((1, D_MODEL)),
        pl.BlockSpec((None, D_MODEL, tf), lambda i, f: (f, 0, 0)),
        pl.BlockSpec((tf, D_MODEL), lambda i, f: (f, 0)),
        _const_spec((1, D_MODEL)),
    ]
    return pl.pallas_call(
        _mlp_kernel,
        grid=(tokens // tm, D_FF // tf),
        in_specs=in_specs,
        out_specs=pl.BlockSpec((tm, D_MODEL), lambda i, f: (i, 0)),
        out_shape=jax.ShapeDtypeStruct((tokens, D_MODEL), jnp.float32),
        scratch_shapes=[pltpu.VMEM((tm, D_MODEL), jnp.bfloat16)],
        compiler_params=pltpu.CompilerParams(
            dimension_semantics=("arbitrary", "arbitrary"), vmem_limit_bytes=VMEM_LIMIT),
        name="mlp",
    )(x, mix, w_out, norm_g, w_up, w_down, norm_final_g)


def _mlp_cast_kernel(x_ref, mix_ref, wout_ref, g_ref, wup_ref, wdown_ref, gfin_ref,
                     o_ref, wup_bf_ref, wdown_bf_ref, xn_ref):
    f32, bf16 = jnp.float32, jnp.bfloat16
    f = pl.program_id(0)
    wup = wup_ref[...].astype(bf16)
    wdown = wdown_ref[...].astype(bf16)
    wup_bf_ref[...] = wup
    wdown_bf_ref[...] = wdown

    @pl.when(f == 0)
    def _():
        x1 = x_ref[...] + jnp.dot(mix_ref[...], wout_ref[...], preferred_element_type=f32)
        o_ref[...] = x1
        r = lax.rsqrt(jnp.mean(x1 * x1, axis=-1, keepdims=True) + EPS)
        xn_ref[...] = ((x1 * r) * g_ref[...]).astype(bf16)

    hm = jnp.dot(xn_ref[...], wup, preferred_element_type=f32)
    act = jnp.square(jnp.maximum(hm, 0.0)).astype(bf16)
    o_ref[...] += jnp.dot(act, wdown, preferred_element_type=f32)

    @pl.when(f == pl.num_programs(0) - 1)
    def _():
        x2 = o_ref[...]
        r = lax.rsqrt(jnp.mean(x2 * x2, axis=-1, keepdims=True) + EPS)
        o_ref[...] = (x2 * r) * gfin_ref[...]


def _mlp_cast(x, mix, w_out, norm_g, w_up_f32, w_down_f32, norm_final_g, *, tf, tf_out):
    tokens = x.shape[0]
    assert D_FF % tf_out == 0 and tf_out % tf == 0
    per_tile = tf_out // tf
    bf16 = jnp.bfloat16
    in_specs = [
        _const_spec((tokens, D_MODEL)),
        _const_spec((tokens, D_MODEL)),
        _const_spec((D_MODEL, D_MODEL)),
        _const_spec((1, D_MODEL)),
        pl.BlockSpec((D_MODEL, tf), lambda f: (0, f)),
        pl.BlockSpec((tf, D_MODEL), lambda f: (f, 0)),
        _const_spec((1, D_MODEL)),
    ]
    out_specs = [
        pl.BlockSpec((tokens, D_MODEL), lambda f: (0, 0)),
        pl.BlockSpec((None, D_MODEL, tf), lambda f: (f // per_tile, 0, f % per_tile)),
        pl.BlockSpec((tf, D_MODEL), lambda f: (f, 0)),
    ]
    out_shape = (
        jax.ShapeDtypeStruct((tokens, D_MODEL), jnp.float32),
        jax.ShapeDtypeStruct((D_FF // tf_out, D_MODEL, tf_out), bf16),
        jax.ShapeDtypeStruct((D_FF, D_MODEL), bf16),
    )
    return pl.pallas_call(
        _mlp_cast_kernel,
        grid=(D_FF // tf,),
        in_specs=in_specs,
        out_specs=out_specs,
        out_shape=out_shape,
        scratch_shapes=[pltpu.VMEM((tokens, D_MODEL), bf16)],
        compiler_params=pltpu.CompilerParams(
            dimension_semantics=("arbitrary",), vmem_limit_bytes=VMEM_LIMIT),
        name="mlp_cast",
    )(x, mix, w_out, norm_g, w_up_f32, w_down_f32, norm_final_g)


def _state_to_rows(h_re, h_im):
    def one(h):
        nb = h.shape[0]
        h = h.reshape(nb // SEQ_PER_STEP, SEQ_PER_STEP, N_BLOCKS, SLABS, LANES)
        return jnp.transpose(h, (0, 2, 3, 1, 4))
    both = jnp.concatenate([one(h_re), one(h_im)], axis=3)
    return both.reshape(both.shape[0], N_BLOCKS * SLABS, 2 * SEQ_PER_STEP, LANES)


def _rows_to_state(rows):
    nbg = rows.shape[0]
    r = rows.reshape(nbg, N_BLOCKS, SLABS, 2, SEQ_PER_STEP, LANES)
    r = jnp.transpose(r, (3, 0, 4, 1, 2, 5))
    r = r.reshape(2, nbg * SEQ_PER_STEP, N_GROUPS, N_STATE)
    return r[0], r[1]


def _a_tables(a_re, a_im, neg_a_im):
    def rows(t, n):
        return jnp.broadcast_to(t.reshape(N_BLOCKS * SLABS, 1, LANES), (N_BLOCKS * SLABS, n, LANES))
    a1 = rows(a_re, 2 * SEQ_PER_STEP)
    a2 = jnp.concatenate([rows(neg_a_im, SEQ_PER_STEP), rows(a_im, SEQ_PER_STEP)], axis=1)
    return a1, a2


def _run_mixer(x, h_re, h_im, conv_prev, lw, tabs, *, tt):
    bsz, seqlen, _ = x.shape
    nbg = bsz // SEQ_PER_STEP
    x4 = x.reshape(nbg, SEQ_PER_STEP, seqlen, D_MODEL)
    h0 = _state_to_rows(h_re, h_im)
    c0 = jnp.pad(conv_prev, ((0, 0), (SUBLANES - (CONV_WIDTH - 1), 0), (0, 0)))
    c0 = c0.reshape(nbg, SEQ_PER_STEP, SUBLANES, D_CONV)
    mix, hout, cout = _mixer(x4, h0, c0, lw["norm_mix_g"], lw["w_in"], *tabs, lw["d"],
                             lw["w_glu"], lw["b_glu"], lw["w_conv"], lw["b_conv"], tt=tt)
    new_re, new_im = _rows_to_state(hout)
    new_conv = cout.reshape(bsz, SUBLANES, D_CONV)[:, SUBLANES - (CONV_WIDTH - 1):, :]
    return mix.reshape(bsz * seqlen, D_MODEL), new_re, new_im, new_conv


def kernel(x_prompt, x_sample, state_ssm_re, state_ssm_im, cache_conv, norm_mix_g, w_in, ssm_lambda_re, ssm_lambda_im, ssm_log_step, ssm_b_re, ssm_b_im, ssm_c_re, ssm_c_im, ssm_d, w_glu, b_glu, w_conv, b_conv, w_out, norm_mlp_g, w_up, w_down, norm_final_g):
    depth = w_in.shape[0]
    assert depth == 1, "the final norm is fused into the last (only) layer"
    bf16 = jnp.bfloat16
    bp = x_prompt.shape[0]
    a_re, a_im, neg_a_im, wb, cc, k0 = _discretise(
        ssm_lambda_re[0], ssm_lambda_im[0], ssm_log_step[0], ssm_b_re[0], ssm_b_im[0], ssm_c_re[0], ssm_c_im[0])
    tabs = _a_tables(a_re, a_im, neg_a_im) + (wb, cc, k0)
    lw = dict(
        norm_mix_g=norm_mix_g[0].reshape(1, D_MODEL), w_in=w_in[0].astype(bf16),
        d=ssm_d[0].reshape(1, D_SSM), w_glu=w_glu[0].astype(bf16), b_glu=b_glu[0].reshape(1, D_SSM),
        w_conv=w_conv[0], b_conv=b_conv[0].reshape(1, D_CONV), w_out=w_out[0].astype(bf16),
        norm_mlp_g=norm_mlp_g[0].reshape(1, D_MODEL))
    gfin = norm_final_g.reshape(1, D_MODEL)

    zeros_h = jnp.zeros((bp, N_GROUPS, N_STATE), jnp.float32)
    zeros_c = jnp.zeros((bp, CONV_WIDTH - 1, D_CONV), x_prompt.dtype)
    mix_s, sre, sim, scv = _run_mixer(x_sample, state_ssm_re[0], state_ssm_im[0], cache_conv[0], lw, tabs, tt=SAMPLE_TILE_T)
    mix_p, pre, pim, pcv = _run_mixer(x_prompt, zeros_h, zeros_h, zeros_c, lw, tabs, tt=PROMPT_TILE_T)
    ys, w_up_bf, w_down_bf = _mlp_cast(x_sample.reshape(-1, D_MODEL), mix_s, lw["w_out"], lw["norm_mlp_g"],
                                       w_up.reshape(D_MODEL, D_FF), w_down.reshape(D_FF, D_MODEL), gfin,
                                       tf=CAST_TILE_F, tf_out=MLP_TILE_F)
    yp = _mlp(x_prompt.reshape(-1, D_MODEL), mix_p, lw["w_out"], lw["norm_mlp_g"], w_up_bf, w_down_bf, gfin,
              tm=MLP_TILE_M)
    return (yp.reshape(x_prompt.shape), ys.reshape(x_sample.shape), pre[None], pim[None], pcv[None],
            sre[None], sim[None], scv[None])
```

```python
import functools

import jax
import jax.numpy as jnp
from jax import lax
from jax.experimental import pallas as pl
from jax.experimental.pallas import tpu as pltpu

D_MODEL = 2048
D_SSM = 1024
D_CONV = 1024
N_GROUPS = 64
GROUP_W = 16
N_STATE = 64
D_FF = 8192
D_IN = D_SSM + 3 * D_CONV
CONV_WIDTH = 3
EPS = 1e-6
INV_SQRT2 = 0.7071067811865476

SUBLANES = 8
LANES = 128
SEQ_PER_STEP = 4
GROUPS_PER_BLOCK = 16
N_BLOCKS = N_GROUPS // GROUPS_PER_BLOCK
BLOCK_IN = GROUPS_PER_BLOCK * GROUP_W
BLOCK_STATE = GROUPS_PER_BLOCK * N_STATE
SLABS = BLOCK_STATE // LANES
GATE_CHUNK = 256
QUAD = 4
QUAD_ROWS = 2 * SEQ_PER_STEP * QUAD
CH_BLOCK = LANES
N_CH_BLOCKS = D_SSM // CH_BLOCK
GROUPS_PER_CH_BLOCK = CH_BLOCK // GROUP_W
VMEM_LIMIT = 60 * 1024 * 1024
PROMPT_TILE_T = 128
SAMPLE_TILE_T = 64
MLP_TILE_M = 512
MLP_TILE_F = 1024
CAST_TILE_F = 512


def _discretise_kernel(lre_ref, lim_ref, ls_ref, lret_ref, limt_ref, lst_ref, brt_ref, bit_ref, cre_ref, cim_ref,
                       crt_ref, cit_ref,
                       a2re_ref, a2im_ref, na2im_ref, wb_out_ref, cc_out_ref, k0_out_ref,
                       zr_ref, zi_ref, are_ref, aim_ref, art_ref, ait_ref, wb_ref, cc_ref, k0_ref):
    f32 = jnp.float32
    lre = lre_ref[...]
    lim = lim_ref[...]
    delta = jnp.exp(ls_ref[...])
    mag = jnp.exp(lre * delta)
    a_re = mag * jnp.cos(lim * delta)
    a_im = mag * jnp.sin(lim * delta)
    den = lre * lre + lim * lim
    zr_ref[...] = ((a_re - 1.0) * lre + a_im * lim) / den
    zi_ref[...] = (a_im * lre - (a_re - 1.0) * lim) / den
    are_ref[...] = a_re
    aim_ref[...] = a_im
    a2_im = 2.0 * a_re * a_im
    a2re_ref[...] = a_re * a_re - a_im * a_im
    a2im_ref[...] = a2_im
    na2im_ref[...] = -a2_im
    delta_t = jnp.exp(lst_ref[...])
    mag_t = jnp.exp(lret_ref[...] * delta_t)
    art_ref[...] = mag_t * jnp.cos(limt_ref[...] * delta_t)
    ait_ref[...] = mag_t * jnp.sin(limt_ref[...] * delta_t)
    wb_ref[...] = jnp.zeros(wb_ref.shape, f32)
    cc_ref[...] = jnp.zeros(cc_ref.shape, f32)
    k0_ref[...] = jnp.zeros(k0_ref.shape, f32)
    nt_dims = (((1,), (1,)), ((), ()))
    for g in range(N_GROUPS):
        j, gl = divmod(g, GROUPS_PER_BLOCK)
        c, g8 = divmod(g, GROUPS_PER_CH_BLOCK)
        zr = zr_ref[pl.ds(g, 1), :]
        zi = zi_ref[pl.ds(g, 1), :]
        ar = are_ref[pl.ds(g, 1), :]
        ai = aim_ref[pl.ds(g, 1), :]
        br = brt_ref[g]
        bi = bit_ref[g]
        bbr = zr * br - zi * bi
        bbi = zr * bi + zi * br
        rows = slice(g8 * GROUP_W, (g8 + 1) * GROUP_W)
        rows_odd = slice(CH_BLOCK + g8 * GROUP_W, CH_BLOCK + (g8 + 1) * GROUP_W)
        cols = slice(g8 * N_STATE, (g8 + 1) * N_STATE)
        half_w = GROUPS_PER_CH_BLOCK * N_STATE
        cols_im = slice(half_w + g8 * N_STATE, half_w + (g8 + 1) * N_STATE)
        wb_ref[c, rows, cols] = ar * bbr - ai * bbi
        wb_ref[c, rows, cols_im] = ar * bbi + ai * bbr
        wb_ref[c, rows_odd, cols] = bbr
        wb_ref[c, rows_odd, cols_im] = bbi

        cr = cre_ref[g]
        ci = cim_ref[g]
        k0t = (lax.dot_general(bbr, cr, nt_dims, precision=lax.Precision.HIGHEST, preferred_element_type=f32)
               - lax.dot_general(bbi, ci, nt_dims, precision=lax.Precision.HIGHEST, preferred_element_type=f32))
        grows = slice(gl * GROUP_W, (gl + 1) * GROUP_W)
        k0_ref[j, grows, grows] = k0t

        arc = art_ref[:, g:g + 1]
        aic = ait_ref[:, g:g + 1]
        crt = crt_ref[g]
        cit = cit_ref[g]
        srows = slice(gl * N_STATE, (gl + 1) * N_STATE)
        srows_im = slice(BLOCK_STATE + gl * N_STATE, BLOCK_STATE + (gl + 1) * N_STATE)
        ecols = slice(BLOCK_IN + gl * GROUP_W, BLOCK_IN + (gl + 1) * GROUP_W)
        cc_ref[j, srows, grows] = crt
        cc_ref[j, srows_im, grows] = -cit
        cc_ref[j, srows, ecols] = crt * arc - cit * aic
        cc_ref[j, srows_im, ecols] = -(cit * arc + crt * aic)
    wb_out_ref[...] = wb_ref[...].astype(jnp.bfloat16)
    cc_out_ref[...] = cc_ref[...].astype(jnp.bfloat16)
    k0_out_ref[...] = k0_ref[...].astype(jnp.bfloat16)


def _discretise(lam_re, lam_im, log_step, b_re, b_im, c_re, c_im):
    f32, bf16 = jnp.float32, jnp.bfloat16
    brt = jnp.transpose(b_re, (0, 2, 1))
    bit = jnp.transpose(b_im, (0, 2, 1))
    crt = jnp.transpose(c_re, (0, 2, 1))
    cit = jnp.transpose(c_im, (0, 2, 1))
    gp = (N_GROUPS, N_STATE)
    wb_shape = (N_CH_BLOCKS, 2 * CH_BLOCK, 2 * GROUPS_PER_CH_BLOCK * N_STATE)
    cc_shape = (N_BLOCKS, 2 * BLOCK_STATE, 2 * BLOCK_IN)
    k0_shape = (N_BLOCKS, BLOCK_IN, BLOCK_IN)
    out_shape = (
        jax.ShapeDtypeStruct(gp, f32), jax.ShapeDtypeStruct(gp, f32), jax.ShapeDtypeStruct(gp, f32),
        jax.ShapeDtypeStruct(wb_shape, bf16), jax.ShapeDtypeStruct(cc_shape, bf16),
        jax.ShapeDtypeStruct(k0_shape, bf16),
    )
    scratch = [pltpu.VMEM(gp, f32) for _ in range(4)] + [pltpu.VMEM((N_STATE, N_GROUPS), f32) for _ in range(2)] + [
        pltpu.VMEM(wb_shape, f32), pltpu.VMEM(cc_shape, f32), pltpu.VMEM(k0_shape, f32)]
    return pl.pallas_call(
        _discretise_kernel,
        out_shape=out_shape,
        scratch_shapes=scratch,
        compiler_params=pltpu.CompilerParams(vmem_limit_bytes=VMEM_LIMIT),
        name="discretise",
    )(lam_re, lam_im, log_step.reshape(N_GROUPS, 1), lam_re.T, lam_im.T, log_step.reshape(1, N_GROUPS),
      brt, bit, c_re, c_im, crt, cit)


def _mixer_kernel(x_ref, h0_ref, c0_ref, g_ref, win_ref, a1_ref, a2_ref, wb_ref, cc_ref, k0_ref, d_ref,
                  wglu_ref, bglu_ref, wconv_ref, bconv_ref,
                  mix_ref, hout_ref, cout_ref,
                  bu_ref, perm_ref, nat_ref, natz_ref, vbuf_ref, xn_ref, *, tt):
    f32, bf16 = jnp.float32, jnp.bfloat16
    nseq = SEQ_PER_STEP
    m = nseq * tt
    m2 = m // 2
    nth = tt // SUBLANES
    half = nseq * QUAD
    ttp = tt + 2 * SUBLANES
    ti = pl.program_id(1)

    @pl.when(ti == 0)
    def _():
        hout_ref[...] = h0_ref[...]
        vbuf_ref[:, 0:SUBLANES, :] = c0_ref[...]
        natz_ref[...] = jnp.zeros(natz_ref.shape, f32)
        for j in range(N_BLOCKS):
            slabs = [h0_ref[j * SLABS + k] for k in range(SLABS)]
            lhs = jnp.concatenate(slabs + [pltpu.roll(sl, nseq, axis=0) for sl in slabs], axis=1).astype(bf16)
            z0 = jnp.dot(lhs, cc_ref[j, :, BLOCK_IN:2 * BLOCK_IN], preferred_element_type=f32)
            for hb in range(2):
                for b in range(nseq):
                    natz_ref[2 * j + hb, pl.ds(b * ttp, 1), :] = z0[b:b + 1, hb * CH_BLOCK:(hb + 1) * CH_BLOCK]

    for hb in range(2):
        rows = slice(hb * m2, (hb + 1) * m2)
        x = x_ref[2 * hb:2 * hb + 2].reshape(m2, D_MODEL)
        r = lax.rsqrt(jnp.mean(x * x, axis=-1, keepdims=True) + EPS)
        xn_ref[rows, :] = ((x * r) * g_ref[...]).astype(bf16)
        u = jnp.dot(xn_ref[rows, :], win_ref[:, 0:D_SSM], preferred_element_type=f32)
        for c in range(N_CH_BLOCKS):
            nat_ref[c, rows, :] = u[:, c * CH_BLOCK:(c + 1) * CH_BLOCK]

    low_rows = lax.broadcasted_iota(jnp.int32, (SUBLANES, LANES), 0) < QUAD

    def halves(first, second, high):
        if high:
            return jnp.where(low_rows, pltpu.roll(first, QUAD, axis=0), second)
        return jnp.where(low_rows, first, pltpu.roll(second, QUAD, axis=0))

    def strided(c, b, th2, parity):
        return pl.ds(b * tt + 2 * SUBLANES * th2 + parity, SUBLANES, stride=2)

    def gather(c, parity):
        out = [None] * (2 * nth)
        for th2 in range(nth // 2):
            for pr in range(nseq // 2):
                e0 = nat_ref[c, strided(c, 2 * pr, th2, parity), :]
                e1 = nat_ref[c, strided(c, 2 * pr + 1, th2, parity), :]
                out[(2 * th2) * 2 + pr] = halves(e0, e1, False)
                out[(2 * th2 + 1) * 2 + pr] = halves(e0, e1, True)
        return jnp.concatenate(out, axis=0)

    sections = 4
    pairs = tt // 2 // sections
    per_gate = D_CONV // GATE_CHUNK
    assert per_gate == N_BLOCKS

    def gate_cols(gate, c):
        start = D_SSM + gate * D_CONV + c * GATE_CHUNK
        return slice(start, start + GATE_CHUNK)

    def project_b(j):
        half_slabs = SLABS // 2
        half_w = GROUPS_PER_CH_BLOCK * N_STATE
        for hb in range(2):
            c = 2 * j + hb
            ccols = slice(c * CH_BLOCK, (c + 1) * CH_BLOCK)
            ev = gather(c, 0)
            od = gather(c, 1)
            perm_ref[0, :, ccols] = ev
            perm_ref[1, :, ccols] = od
            lhs = jnp.concatenate([ev, od], axis=1).astype(bf16)
            v = jnp.dot(lhs, wb_ref[c], preferred_element_type=f32)
            for kk in range(half_slabs):
                k = hb * half_slabs + kk
                re = v[:, kk * LANES:(kk + 1) * LANES]
                im = v[:, half_w + kk * LANES:half_w + (kk + 1) * LANES]
                bu_ref[j % 2, k, :, 0:half, :] = re.reshape(nth, half, LANES)
                bu_ref[j % 2, k, :, half:2 * half, :] = im.reshape(nth, half, LANES)

    def advance(j, q, hs, a1, a2):
        buf = j % 2
        for p in range(q * pairs, (q + 1) * pairs):
            th, p4 = divmod(p, QUAD)
            rows = pl.ds(p4, 2 * nseq, stride=QUAD)
            for k in range(SLABS):
                h = hs[k]
                nh = a1[k] * h + a2[k] * pltpu.roll(h, nseq, axis=0) + bu_ref[buf, k, th, rows, :]
                bu_ref[buf, k, th, rows, :] = nh
                hs[k] = nh
        return hs

    project_b(0)
    for j in range(N_BLOCKS):
        if j + 1 < N_BLOCKS:
            project_b(j + 1)
        a1 = [a1_ref[j * SLABS + k] for k in range(SLABS)]
        a2 = [a2_ref[j * SLABS + k] for k in range(SLABS)]
        hs = [hout_ref[j * SLABS + k] for k in range(SLABS)]

        cols = slice(j * GATE_CHUNK, (j + 1) * GATE_CHUNK)
        gate_b = jnp.dot(xn_ref[...], win_ref[:, gate_cols(0, j)], preferred_element_type=f32)
        hs = advance(j, 0, hs, a1, a2)
        gate_c = jnp.dot(xn_ref[...], win_ref[:, gate_cols(1, j)], preferred_element_type=f32)
        hs = advance(j, 1, hs, a1, a2)
        xv = jnp.dot(xn_ref[...], win_ref[:, gate_cols(2, j)], preferred_element_type=f32)
        hs = advance(j, 2, hs, a1, a2)
        v = gate_c * xv
        vbuf_ref[:, SUBLANES:, cols] = v.reshape(nseq, tt, GATE_CHUNK)
        v1 = vbuf_ref[:, SUBLANES - 1:SUBLANES - 1 + tt, cols].reshape(m, GATE_CHUNK)
        v2 = vbuf_ref[:, SUBLANES - 2:SUBLANES - 2 + tt, cols].reshape(m, GATE_CHUNK)
        yc = (bconv_ref[:, cols] + wconv_ref[0:1, cols] * v2 + wconv_ref[1:2, cols] * v1
              + wconv_ref[2:3, cols] * v)
        mix_ref[:, :, D_SSM + j * GATE_CHUNK:D_SSM + (j + 1) * GATE_CHUNK] = (
            (gate_b * yc).astype(bf16).reshape(nseq, tt, GATE_CHUNK))
        hs = advance(j, 3, hs, a1, a2)

        for k in range(SLABS):
            hout_ref[j * SLABS + k] = hs[k]
        h_all = jnp.concatenate(
            [bu_ref[j % 2, k, :, 0:half, :].reshape(m2, LANES) for k in range(SLABS)]
            + [bu_ref[j % 2, k, :, half:2 * half, :].reshape(m2, LANES) for k in range(SLABS)],
            axis=1).astype(bf16)
        y2 = jnp.dot(h_all, cc_ref[j], preferred_element_type=f32)
        ucols = slice(j * BLOCK_IN, (j + 1) * BLOCK_IN)
        u_even = perm_ref[0, :, ucols]
        u_odd = perm_ref[1, :, ucols]
        feed = jnp.dot(u_even.astype(bf16), k0_ref[j], preferred_element_type=f32)
        perm_ref[0, :, ucols] = feed + d_ref[:, ucols] * u_even
        perm_ref[1, :, ucols] = y2[:, 0:BLOCK_IN] + d_ref[:, ucols] * u_odd
        for hb in range(2):
            zc = y2[:, BLOCK_IN + hb * CH_BLOCK:BLOCK_IN + (hb + 1) * CH_BLOCK]
            for th2 in range(nth // 2):
                for pr in range(nseq // 2):
                    r0 = ((2 * th2) * 2 + pr) * SUBLANES
                    r1 = ((2 * th2 + 1) * 2 + pr) * SUBLANES
                    for odd_seq in (False, True):
                        start = (2 * pr + odd_seq) * ttp + 2 * SUBLANES * th2 + 2
                        natz_ref[2 * j + hb, pl.ds(start, SUBLANES, stride=2), :] = halves(
                            zc[r0:r0 + SUBLANES], zc[r1:r1 + SUBLANES], odd_seq)

    tail = vbuf_ref[:, tt:tt + SUBLANES, :]
    vbuf_ref[:, 0:SUBLANES, :] = tail
    cout_ref[...] = tail

    for c in range(N_CH_BLOCKS):
        ccols = slice(c * CH_BLOCK, (c + 1) * CH_BLOCK)
        for parity in range(2):
            for th2 in range(nth // 2):
                for pr in range(nseq // 2):
                    r0 = ((2 * th2) * 2 + pr) * SUBLANES
                    r1 = ((2 * th2 + 1) * 2 + pr) * SUBLANES
                    first = perm_ref[parity, r0:r0 + SUBLANES, ccols]
                    second = perm_ref[parity, r1:r1 + SUBLANES, ccols]
                    nat_ref[c, strided(c, 2 * pr, th2, parity), :] = halves(first, second, False)
                    nat_ref[c, strided(c, 2 * pr + 1, th2, parity), :] = halves(first, second, True)
    for hb in range(2):
        rows = slice(hb * m2, (hb + 1) * m2)
        y = jnp.concatenate([nat_ref[c, rows, :] for c in range(N_CH_BLOCKS)], axis=1)
        y = y + jnp.concatenate(
            [jnp.concatenate([natz_ref[c, b * ttp:b * ttp + tt, :] for b in (2 * hb, 2 * hb + 1)], axis=0)
             for c in range(N_CH_BLOCKS)], axis=1)
        y = 0.5 * y * (1.0 + lax.erf(y * INV_SQRT2))
        gl = jnp.dot(y.astype(bf16), wglu_ref[...], preferred_element_type=f32) + bglu_ref[...]
        y_ssm = y * jax.nn.sigmoid(gl)
        mix_ref[2 * hb:2 * hb + 2, :, 0:D_SSM] = y_ssm.astype(bf16).reshape(nseq // 2, tt, D_SSM)
    for c in range(N_CH_BLOCKS):
        for b in range(nseq):
            natz_ref[c, pl.ds(b * ttp, 1), :] = natz_ref[c, pl.ds(b * ttp + tt, 1), :]


def _const_spec(shape):
    zeros = (0,) * len(shape)
    return pl.BlockSpec(shape, lambda *_: zeros, pipeline_mode=pl.Buffered(1))


def _mixer(x, h0, c0, norm_g, w_in, a1_tab, a2_tab, wb, cc, k0, d, w_glu, b_glu, w_conv, b_conv, *, tt):
    nbg, nseq, seqlen, _ = x.shape
    assert nseq == SEQ_PER_STEP and seqlen % tt == 0 and tt % (8 * SUBLANES) == 0
    nt = seqlen // tt
    m = nseq * tt
    f32 = jnp.float32
    nstate_rows = N_BLOCKS * SLABS
    in_specs = [
        pl.BlockSpec((None, nseq, tt, D_MODEL), lambda bg, ti: (bg, 0, ti, 0)),
        pl.BlockSpec((None, nstate_rows, SUBLANES, LANES), lambda bg, ti: (bg, 0, 0, 0)),
        pl.BlockSpec((None, nseq, SUBLANES, D_CONV), lambda bg, ti: (bg, 0, 0, 0)),
        _const_spec((1, D_MODEL)),
        _const_spec((D_MODEL, D_IN)),
        _const_spec((nstate_rows, SUBLANES, LANES)),
        _const_spec((nstate_rows, SUBLANES, LANES)),
        _const_spec((N_CH_BLOCKS, 2 * CH_BLOCK, 2 * GROUPS_PER_CH_BLOCK * N_STATE)),
        _const_spec((N_BLOCKS, 2 * BLOCK_STATE, 2 * BLOCK_IN)),
        _const_spec((N_BLOCKS, BLOCK_IN, BLOCK_IN)),
        _const_spec((1, D_SSM)),
        _const_spec((D_SSM, D_SSM)),
        _const_spec((1, D_SSM)),
        _const_spec((CONV_WIDTH, D_CONV)),
        _const_spec((1, D_CONV)),
    ]
    out_specs = [
        pl.BlockSpec((None, nseq, tt, D_MODEL), lambda bg, ti: (bg, 0, ti, 0)),
        pl.BlockSpec((None, nstate_rows, SUBLANES, LANES), lambda bg, ti: (bg, 0, 0, 0)),
        pl.BlockSpec((None, nseq, SUBLANES, D_CONV), lambda bg, ti: (bg, 0, 0, 0)),
    ]
    out_shape = (
        jax.ShapeDtypeStruct((nbg, nseq, seqlen, D_MODEL), jnp.bfloat16),
        jax.ShapeDtypeStruct((nbg, nstate_rows, SUBLANES, LANES), f32),
        jax.ShapeDtypeStruct((nbg, nseq, SUBLANES, D_CONV), f32),
    )
    scratch = [
        pltpu.VMEM((2, SLABS, tt // SUBLANES, QUAD_ROWS, LANES), f32),
        pltpu.VMEM((2, m // 2, D_SSM), f32),
        pltpu.VMEM((N_CH_BLOCKS, m, CH_BLOCK), f32),
        pltpu.VMEM((N_CH_BLOCKS, nseq * (tt + 2 * SUBLANES), CH_BLOCK), f32),
        pltpu.VMEM((nseq, tt + SUBLANES, D_CONV), f32),
        pltpu.VMEM((m, D_MODEL), jnp.bfloat16),
    ]
    return pl.pallas_call(
        functools.partial(_mixer_kernel, tt=tt),
        grid=(nbg, nt),
        in_specs=in_specs,
        out_specs=out_specs,
        out_shape=out_shape,
        scratch_shapes=scratch,
        compiler_params=pltpu.CompilerParams(
            dimension_semantics=("arbitrary", "arbitrary"), vmem_limit_bytes=VMEM_LIMIT),
        name="mixer",
    )(x, h0, c0, norm_g, w_in, a1_tab, a2_tab, wb, cc, k0, d, w_glu, b_glu, w_conv, b_conv)


def _mlp_kernel(x_ref, mix_ref, wout_ref, g_ref, wup_ref, wdown_ref, gfin_ref, o_ref, xn_ref):
    f32, bf16 = jnp.float32, jnp.bfloat16
    f = pl.program_id(1)

    @pl.when(f == 0)
    def _():
        x1 = x_ref[...] + jnp.dot(mix_ref[...], wout_ref[...], preferred_element_type=f32)
        o_ref[...] = x1
        r = lax.rsqrt(jnp.mean(x1 * x1, axis=-1, keepdims=True) + EPS)
        xn_ref[...] = ((x1 * r) * g_ref[...]).astype(bf16)

    hm = jnp.dot(xn_ref[...], wup_ref[...], preferred_element_type=f32)
    act = jnp.square(jnp.maximum(hm, 0.0)).astype(bf16)
    o_ref[...] += jnp.dot(act, wdown_ref[...], preferred_element_type=f32)

    @pl.when(f == pl.num_programs(1) - 1)
    def _():
        x2 = o_ref[...]
        r = lax.rsqrt(jnp.mean(x2 * x2, axis=-1, keepdims=True) + EPS)
        o_ref[...] = (x2 * r) * gfin_ref[...]


def _mlp(x, mix, w_out, norm_g, w_up, w_down, norm_final_g, *, tm):
    tokens = x.shape[0]
    tf = w_up.shape[2]
    assert tokens % tm == 0 and w_up.shape == (D_FF // tf, D_MODEL, tf)
    in_specs = [
        pl.BlockSpec((tm, D_MODEL), lambda i, f: (i, 0)),
        pl.BlockSpec((tm, D_MODEL), lambda i, f: (i, 0)),
        _const_spec((D_MODEL, D_MODEL)),
        _const_spec((1, D_MODEL)),
        pl.BlockSpec((None, D_MODEL, tf), lambda i, f: (f, 0, 0)),
        pl.BlockSpec((tf, D_MODEL), lambda i, f: (f, 0)),
        _const_spec((1, D_MODEL)),
    ]
    return pl.pallas_call(
        _mlp_kernel,
        grid=(tokens // tm, D_FF // tf),
        in_specs=in_specs,
        out_specs=pl.BlockSpec((tm, D_MODEL), lambda i, f: (i, 0)),
        out_shape=jax.ShapeDtypeStruct((tokens, D_MODEL), jnp.float32),
        scratch_shapes=[pltpu.VMEM((tm, D_MODEL), jnp.bfloat16)],
        compiler_params=pltpu.CompilerParams(
            dimension_semantics=("arbitrary", "arbitrary"), vmem_limit_bytes=VMEM_LIMIT),
        name="mlp",
    )(x, mix, w_out, norm_g, w_up, w_down, norm_final_g)


def _mlp_cast_kernel(x_ref, mix_ref, wout_ref, g_ref, wup_ref, wdown_ref, gfin_ref,
                     o_ref, wup_bf_ref, wdown_bf_ref, xn_ref):
    f32, bf16 = jnp.float32, jnp.bfloat16
    f = pl.program_id(0)
    wup = wup_ref[...].astype(bf16)
    wdown = wdown_ref[...].astype(bf16)
    wup_bf_ref[...] = wup
    wdown_bf_ref[...] = wdown

    @pl.when(f == 0)
    def _():
        x1 = x_ref[...] + jnp.dot(mix_ref[...], wout_ref[...], preferred_element_type=f32)
        o_ref[...] = x1
        r = lax.rsqrt(jnp.mean(x1 * x1, axis=-1, keepdims=True) + EPS)
        xn_ref[...] = ((x1 * r) * g_ref[...]).astype(bf16)

    hm = jnp.dot(xn_ref[...], wup, preferred_element_type=f32)
    act = jnp.square(jnp.maximum(hm, 0.0)).astype(bf16)
    o_ref[...] += jnp.dot(act, wdown, preferred_element_type=f32)

    @pl.when(f == pl.num_programs(0) - 1)
    def _():
        x2 = o_ref[...]
        r = lax.rsqrt(jnp.mean(x2 * x2, axis=-1, keepdims=True) + EPS)
        o_ref[...] = (x2 * r) * gfin_ref[...]


def _mlp_cast(x, mix, w_out, norm_g, w_up_f32, w_down_f32, norm_final_g, *, tf, tf_out):
    tokens = x.shape[0]
    assert D_FF % tf_out == 0 and tf_out % tf == 0
    per_tile = tf_out // tf
    bf16 = jnp.bfloat16
    in_specs = [
        _const_spec((tokens, D_MODEL)),
        _const_spec((tokens, D_MODEL)),
        _const_spec((D_MODEL, D_MODEL)),
        _const_spec((1, D_MODEL)),
        pl.BlockSpec((D_MODEL, tf), lambda f: (0, f)),
        pl.BlockSpec((tf, D_MODEL), lambda f: (f, 0)),
        _const_spec((1, D_MODEL)),
    ]
    out_specs = [
        pl.BlockSpec((tokens, D_MODEL), lambda f: (0, 0)),
        pl.BlockSpec((None, D_MODEL, tf), lambda f: (f // per_tile, 0, f % per_tile)),
        pl.BlockSpec((tf, D_MODEL), lambda f: (f, 0)),
    ]
    out_shape = (
        jax.ShapeDtypeStruct((tokens, D_MODEL), jnp.float32),
        jax.ShapeDtypeStruct((D_FF // tf_out, D_MODEL, tf_out), bf16),
        jax.ShapeDtypeStruct((D_FF, D_MODEL), bf16),
    )
    return pl.pallas_call(
        _mlp_cast_kernel,
        grid=(D_FF // tf,),
        in_specs=in_specs,
        out_specs=out_specs,
        out_shape=out_shape,
        scratch_shapes=[pltpu.VMEM((tokens, D_MODEL), bf16)],
        compiler_params=pltpu.CompilerParams(
            dimension_semantics=("arbitrary",), vmem_limit_bytes=VMEM_LIMIT),
        name="mlp_cast",
    )(x, mix, w_out, norm_g, w_up_f32, w_down_f32, norm_final_g)


def _state_to_rows(h_re, h_im):
    def one(h):
        nb = h.shape[0]
        h = h.reshape(nb // SEQ_PER_STEP, SEQ_PER_STEP, N_BLOCKS, SLABS, LANES)
        return jnp.transpose(h, (0, 2, 3, 1, 4))
    both = jnp.concatenate([one(h_re), one(h_im)], axis=3)
    return both.reshape(both.shape[0], N_BLOCKS * SLABS, 2 * SEQ_PER_STEP, LANES)


def _rows_to_state(rows):
    nbg = rows.shape[0]
    r = rows.reshape(nbg, N_BLOCKS, SLABS, 2, SEQ_PER_STEP, LANES)
    r = jnp.transpose(r, (3, 0, 4, 1, 2, 5))
    r = r.reshape(2, nbg * SEQ_PER_STEP, N_GROUPS, N_STATE)
    return r[0], r[1]


def _a_tables(a_re, a_im, neg_a_im):
    def rows(t, n):
        return jnp.broadcast_to(t.reshape(N_BLOCKS * SLABS, 1, LANES), (N_BLOCKS * SLABS, n, LANES))
    a1 = rows(a_re, 2 * SEQ_PER_STEP)
    a2 = jnp.concatenate([rows(neg_a_im, SEQ_PER_STEP), rows(a_im, SEQ_PER_STEP)], axis=1)
    return a1, a2


def _run_mixer(x, h_re, h_im, conv_prev, lw, tabs, *, tt):
    bsz, seqlen, _ = x.shape
    nbg = bsz // SEQ_PER_STEP
    x4 = x.reshape(nbg, SEQ_PER_STEP, seqlen, D_MODEL)
    h0 = _state_to_rows(h_re, h_im)
    c0 = jnp.pad(conv_prev, ((0, 0), (SUBLANES - (CONV_WIDTH - 1), 0), (0, 0)))
    c0 = c0.reshape(nbg, SEQ_PER_STEP, SUBLANES, D_CONV)
    mix, hout, cout = _mixer(x4, h0, c0, lw["norm_mix_g"], lw["w_in"], *tabs, lw["d"],
                             lw["w_glu"], lw["b_glu"], lw["w_conv"], lw["b_conv"], tt=tt)
    new_re, new_im = _rows_to_state(hout)
    new_conv = cout.reshape(bsz, SUBLANES, D_CONV)[:, SUBLANES - (CONV_WIDTH - 1):, :]
    return mix.reshape(bsz * seqlen, D_MODEL), new_re, new_im, new_conv


def kernel(x_prompt, x_sample, state_ssm_re, state_ssm_im, cache_conv, norm_mix_g, w_in, ssm_lambda_re, ssm_lambda_im, ssm_log_step, ssm_b_re, ssm_b_im, ssm_c_re, ssm_c_im, ssm_d, w_glu, b_glu, w_conv, b_conv, w_out, norm_mlp_g, w_up, w_down, norm_final_g):
    depth = w_in.shape[0]
    assert depth == 1, "the final norm is fused into the last (only) layer"
    bf16 = jnp.bfloat16
    bp = x_prompt.shape[0]
    a_re, a_im, neg_a_im, wb, cc, k0 = _discretise(
        ssm_lambda_re[0], ssm_lambda_im[0], ssm_log_step[0], ssm_b_re[0], ssm_b_im[0], ssm_c_re[0], ssm_c_im[0])
    tabs = _a_tables(a_re, a_im, neg_a_im) + (wb, cc, k0)
    lw = dict(
        norm_mix_g=norm_mix_g[0].reshape(1, D_MODEL), w_in=w_in[0].astype(bf16),
        d=ssm_d[0].reshape(1, D_SSM), w_glu=w_glu[0].astype(bf16), b_glu=b_glu[0].reshape(1, D_SSM),
        w_conv=w_conv[0], b_conv=b_conv[0].reshape(1, D_CONV), w_out=w_out[0].astype(bf16),
        norm_mlp_g=norm_mlp_g[0].reshape(1, D_MODEL))
    gfin = norm_final_g.reshape(1, D_MODEL)

    zeros_h = jnp.zeros((bp, N_GROUPS, N_STATE), jnp.float32)
    zeros_c = jnp.zeros((bp, CONV_WIDTH - 1, D_CONV), x_prompt.dtype)
    mix_s, sre, sim, scv = _run_mixer(x_sample, state_ssm_re[0], state_ssm_im[0], cache_conv[0], lw, tabs, tt=SAMPLE_TILE_T)
    mix_p, pre, pim, pcv = _run_mixer(x_prompt, zeros_h, zeros_h, zeros_c, lw, tabs, tt=PROMPT_TILE_T)
    ys, w_up_bf, w_down_bf = _mlp_cast(x_sample.reshape(-1, D_MODEL), mix_s, lw["w_out"], lw["norm_mlp_g"],
                                       w_up.reshape(D_MODEL, D_FF), w_down.reshape(D_FF, D_MODEL), gfin,
                                       tf=CAST_TILE_F, tf_out=MLP_TILE_F)
    yp = _mlp(x_prompt.reshape(-1, D_MODEL), mix_p, lw["w_out"], lw["norm_mlp_g"], w_up_bf, w_down_bf, gfin,
              tm=MLP_TILE_M)
    return (yp.reshape(x_prompt.shape), ys.reshape(x_sample.shape), pre[None], pim[None], pcv[None],
            sre[None], sim[None], scv[None])
```

```python
import functools

import jax
import jax.numpy as jnp
from jax import lax
from jax.experimental import pallas as pl
from jax.experimental.pallas import tpu as pltpu

D_MODEL = 2048
D_SSM = 1024
D_CONV = 1024
N_GROUPS = 64
GROUP_W = 16
N_STATE = 64
D_FF = 8192
D_IN = D_SSM + 3 * D_CONV
CONV_WIDTH = 3
EPS = 1e-6
INV_SQRT2 = 0.7071067811865476

SUBLANES = 8
LANES = 128
SEQ_PER_STEP = 4
GROUPS_PER_BLOCK = 16
N_BLOCKS = N_GROUPS // GROUPS_PER_BLOCK
BLOCK_IN = GROUPS_PER_BLOCK * GROUP_W
BLOCK_STATE = GROUPS_PER_BLOCK * N_STATE
SLABS = BLOCK_STATE // LANES
GATE_CHUNK = 256
QUAD = 4
QUAD_ROWS = 2 * SEQ_PER_STEP * QUAD
CH_BLOCK = LANES
N_CH_BLOCKS = D_SSM // CH_BLOCK
GROUPS_PER_CH_BLOCK = CH_BLOCK // GROUP_W
VMEM_LIMIT = 60 * 1024 * 1024
PROMPT_TILE_T = 128
SAMPLE_TILE_T = 64
MLP_TILE_M = 512
MLP_TILE_F = 2048
MLP_DOT_F = 1024
CAST_TILE_F = 512


def _discretise_kernel(lre_ref, lim_ref, ls_ref, lret_ref, limt_ref, lst_ref, brt_ref, bit_ref, cre_ref, cim_ref,
                       crt_ref, cit_ref,
                       a2re_ref, a2im_ref, na2im_ref, wb_out_ref, cc_out_ref, k0_out_ref,
                       zr_ref, zi_ref, are_ref, aim_ref, art_ref, ait_ref, wb_ref, cc_ref, k0_ref):
    f32 = jnp.float32
    lre = lre_ref[...]
    lim = lim_ref[...]
    delta = jnp.exp(ls_ref[...])
    mag = jnp.exp(lre * delta)
    a_re = mag * jnp.cos(lim * delta)
    a_im = mag * jnp.sin(lim * delta)
    den = lre * lre + lim * lim
    zr_ref[...] = ((a_re - 1.0) * lre + a_im * lim) / den
    zi_ref[...] = (a_im * lre - (a_re - 1.0) * lim) / den
    are_ref[...] = a_re
    aim_ref[...] = a_im
    a2_im = 2.0 * a_re * a_im
    a2re_ref[...] = a_re * a_re - a_im * a_im
    a2im_ref[...] = a2_im
    na2im_ref[...] = -a2_im
    delta_t = jnp.exp(lst_ref[...])
    mag_t = jnp.exp(lret_ref[...] * delta_t)
    art_ref[...] = mag_t * jnp.cos(limt_ref[...] * delta_t)
    ait_ref[...] = mag_t * jnp.sin(limt_ref[...] * delta_t)
    wb_ref[...] = jnp.zeros(wb_ref.shape, f32)
    cc_ref[...] = jnp.zeros(cc_ref.shape, f32)
    k0_ref[...] = jnp.zeros(k0_ref.shape, f32)
    nt_dims = (((1,), (1,)), ((), ()))
    for g in range(N_GROUPS):
        j, gl = divmod(g, GROUPS_PER_BLOCK)
        c, g8 = divmod(g, GROUPS_PER_CH_BLOCK)
        zr = zr_ref[pl.ds(g, 1), :]
        zi = zi_ref[pl.ds(g, 1), :]
        ar = are_ref[pl.ds(g, 1), :]
        ai = aim_ref[pl.ds(g, 1), :]
        br = brt_ref[g]
        bi = bit_ref[g]
        bbr = zr * br - zi * bi
        bbi = zr * bi + zi * br
        rows = slice(g8 * GROUP_W, (g8 + 1) * GROUP_W)
        rows_odd = slice(CH_BLOCK + g8 * GROUP_W, CH_BLOCK + (g8 + 1) * GROUP_W)
        cols = slice(g8 * N_STATE, (g8 + 1) * N_STATE)
        half_w = GROUPS_PER_CH_BLOCK * N_STATE
        cols_im = slice(half_w + g8 * N_STATE, half_w + (g8 + 1) * N_STATE)
        wb_ref[c, rows, cols] = ar * bbr - ai * bbi
        wb_ref[c, rows, cols_im] = ar * bbi + ai * bbr
        wb_ref[c, rows_odd, cols] = bbr
        wb_ref[c, rows_odd, cols_im] = bbi

        cr = cre_ref[g]
        ci = cim_ref[g]
        k0t = (lax.dot_general(bbr, cr, nt_dims, precision=lax.Precision.HIGHEST, preferred_element_type=f32)
               - lax.dot_general(bbi, ci, nt_dims, precision=lax.Precision.HIGHEST, preferred_element_type=f32))
        grows = slice(gl * GROUP_W, (gl + 1) * GROUP_W)
        k0_ref[j, grows, grows] = k0t

        arc = art_ref[:, g:g + 1]
        aic = ait_ref[:, g:g + 1]
        crt = crt_ref[g]
        cit = cit_ref[g]
        srows = slice(gl * N_STATE, (gl + 1) * N_STATE)
        srows_im = slice(BLOCK_STATE + gl * N_STATE, BLOCK_STATE + (gl + 1) * N_STATE)
        ecols = slice(BLOCK_IN + gl * GROUP_W, BLOCK_IN + (gl + 1) * GROUP_W)
        cc_ref[j, srows, grows] = crt
        cc_ref[j, srows_im, grows] = -cit
        cc_ref[j, srows, ecols] = crt * arc - cit * aic
        cc_ref[j, srows_im, ecols] = -(cit * arc + crt * aic)
    wb_out_ref[...] = wb_ref[...].astype(jnp.bfloat16)
    cc_out_ref[...] = cc_ref[...].astype(jnp.bfloat16)
    k0_out_ref[...] = k0_ref[...].astype(jnp.bfloat16)


def _discretise(lam_re, lam_im, log_step, b_re, b_im, c_re, c_im):
    f32, bf16 = jnp.float32, jnp.bfloat16
    brt = jnp.transpose(b_re, (0, 2, 1))
    bit = jnp.transpose(b_im, (0, 2, 1))
    crt = jnp.transpose(c_re, (0, 2, 1))
    cit = jnp.transpose(c_im, (0, 2, 1))
    gp = (N_GROUPS, N_STATE)
    wb_shape = (N_CH_BLOCKS, 2 * CH_BLOCK, 2 * GROUPS_PER_CH_BLOCK * N_STATE)
    cc_shape = (N_BLOCKS, 2 * BLOCK_STATE, 2 * BLOCK_IN)
    k0_shape = (N_BLOCKS, BLOCK_IN, BLOCK_IN)
    out_shape = (
        jax.ShapeDtypeStruct(gp, f32), jax.ShapeDtypeStruct(gp, f32), jax.ShapeDtypeStruct(gp, f32),
        jax.ShapeDtypeStruct(wb_shape, bf16), jax.ShapeDtypeStruct(cc_shape, bf16),
        jax.ShapeDtypeStruct(k0_shape, bf16),
    )
    scratch = [pltpu.VMEM(gp, f32) for _ in range(4)] + [pltpu.VMEM((N_STATE, N_GROUPS), f32) for _ in range(2)] + [
        pltpu.VMEM(wb_shape, f32), pltpu.VMEM(cc_shape, f32), pltpu.VMEM(k0_shape, f32)]
    return pl.pallas_call(
        _discretise_kernel,
        out_shape=out_shape,
        scratch_shapes=scratch,
        compiler_params=pltpu.CompilerParams(vmem_limit_bytes=VMEM_LIMIT),
        name="discretise",
    )(lam_re, lam_im, log_step.reshape(N_GROUPS, 1), lam_re.T, lam_im.T, log_step.reshape(1, N_GROUPS),
      brt, bit, c_re, c_im, crt, cit)


def _mixer_kernel(x_ref, h0_ref, c0_ref, g_ref, win_ref, a1_ref, a2_ref, wb_ref, cc_ref, k0_ref, d_ref,
                  wglu_ref, bglu_ref, wconv_ref, bconv_ref,
                  mix_ref, hout_ref, cout_ref,
                  bu_ref, perm_ref, nat_ref, natz_ref, vbuf_ref, xn_ref, *, tt):
    f32, bf16 = jnp.float32, jnp.bfloat16
    nseq = SEQ_PER_STEP
    m = nseq * tt
    m2 = m // 2
    nth = tt // SUBLANES
    half = nseq * QUAD
    ttp = tt + 2 * SUBLANES
    ti = pl.program_id(1)

    @pl.when(ti == 0)
    def _():
        hout_ref[...] = h0_ref[...]
        vbuf_ref[:, 0:SUBLANES, :] = c0_ref[...]
        natz_ref[...] = jnp.zeros(natz_ref.shape, f32)
        for j in range(N_BLOCKS):
            slabs = [h0_ref[j * SLABS + k] for k in range(SLABS)]
            lhs = jnp.concatenate(slabs + [pltpu.roll(sl, nseq, axis=0) for sl in slabs], axis=1).astype(bf16)
            z0 = jnp.dot(lhs, cc_ref[j, :, BLOCK_IN:2 * BLOCK_IN], preferred_element_type=f32)
            for hb in range(2):
                for b in range(nseq):
                    natz_ref[2 * j + hb, pl.ds(b * ttp, 1), :] = z0[b:b + 1, hb * CH_BLOCK:(hb + 1) * CH_BLOCK]

    for hb in range(2):
        rows = slice(hb * m2, (hb + 1) * m2)
        x = x_ref[2 * hb:2 * hb + 2].reshape(m2, D_MODEL)
        r = lax.rsqrt(jnp.mean(x * x, axis=-1, keepdims=True) + EPS)
        xn_ref[rows, :] = ((x * r) * g_ref[...]).astype(bf16)
        u = jnp.dot(xn_ref[rows, :], win_ref[:, 0:D_SSM], preferred_element_type=f32)
        for c in range(N_CH_BLOCKS):
            nat_ref[c, rows, :] = u[:, c * CH_BLOCK:(c + 1) * CH_BLOCK]

    low_rows = lax.broadcasted_iota(jnp.int32, (SUBLANES, LANES), 0) < QUAD

    def halves(first, second, high):
        if high:
            return jnp.where(low_rows, pltpu.roll(first, QUAD, axis=0), second)
        return jnp.where(low_rows, first, pltpu.roll(second, QUAD, axis=0))

    def strided(c, b, th2, parity):
        return pl.ds(b * tt + 2 * SUBLANES * th2 + parity, SUBLANES, stride=2)

    def gather(c, parity):
        out = [None] * (2 * nth)
        for th2 in range(nth // 2):
            for pr in range(nseq // 2):
                e0 = nat_ref[c, strided(c, 2 * pr, th2, parity), :]
                e1 = nat_ref[c, strided(c, 2 * pr + 1, th2, parity), :]
                out[(2 * th2) * 2 + pr] = halves(e0, e1, False)
                out[(2 * th2 + 1) * 2 + pr] = halves(e0, e1, True)
        return jnp.concatenate(out, axis=0)

    sections = 4
    pairs = tt // 2 // sections
    per_gate = D_CONV // GATE_CHUNK
    assert per_gate == N_BLOCKS

    def gate_cols(gate, c):
        start = D_SSM + gate * D_CONV + c * GATE_CHUNK
        return slice(start, start + GATE_CHUNK)

    def project_b(j):
        half_slabs = SLABS // 2
        half_w = GROUPS_PER_CH_BLOCK * N_STATE
        for hb in range(2):
            c = 2 * j + hb
            ccols = slice(c * CH_BLOCK, (c + 1) * CH_BLOCK)
            ev = gather(c, 0)
            od = gather(c, 1)
            perm_ref[0, :, ccols] = ev
            perm_ref[1, :, ccols] = od
            lhs = jnp.concatenate([ev, od], axis=1).astype(bf16)
            v = jnp.dot(lhs, wb_ref[c], preferred_element_type=f32)
            for kk in range(half_slabs):
                k = hb * half_slabs + kk
                re = v[:, kk * LANES:(kk + 1) * LANES]
                im = v[:, half_w + kk * LANES:half_w + (kk + 1) * LANES]
                bu_ref[j % 2, k, :, 0:half, :] = re.reshape(nth, half, LANES)
                bu_ref[j % 2, k, :, half:2 * half, :] = im.reshape(nth, half, LANES)

    def advance(j, q, hs, a1, a2):
        buf = j % 2
        for p in range(q * pairs, (q + 1) * pairs):
            th, p4 = divmod(p, QUAD)
            rows = pl.ds(p4, 2 * nseq, stride=QUAD)
            for k in range(SLABS):
                h = hs[k]
                nh = a1[k] * h + a2[k] * pltpu.roll(h, nseq, axis=0) + bu_ref[buf, k, th, rows, :]
                bu_ref[buf, k, th, rows, :] = nh
                hs[k] = nh
        return hs

    project_b(0)
    for j in range(N_BLOCKS):
        if j + 1 < N_BLOCKS:
            project_b(j + 1)
        a1 = [a1_ref[j * SLABS + k] for k in range(SLABS)]
        a2 = [a2_ref[j * SLABS + k] for k in range(SLABS)]
        hs = [hout_ref[j * SLABS + k] for k in range(SLABS)]

        cols = slice(j * GATE_CHUNK, (j + 1) * GATE_CHUNK)
        gate_b = jnp.dot(xn_ref[...], win_ref[:, gate_cols(0, j)], preferred_element_type=f32)
        hs = advance(j, 0, hs, a1, a2)
        gate_c = jnp.dot(xn_ref[...], win_ref[:, gate_cols(1, j)], preferred_element_type=f32)
        hs = advance(j, 1, hs, a1, a2)
        xv = jnp.dot(xn_ref[...], win_ref[:, gate_cols(2, j)], preferred_element_type=f32)
        hs = advance(j, 2, hs, a1, a2)
        v = gate_c * xv
        vbuf_ref[:, SUBLANES:, cols] = v.reshape(nseq, tt, GATE_CHUNK)
        v1 = vbuf_ref[:, SUBLANES - 1:SUBLANES - 1 + tt, cols].reshape(m, GATE_CHUNK)
        v2 = vbuf_ref[:, SUBLANES - 2:SUBLANES - 2 + tt, cols].reshape(m, GATE_CHUNK)
        yc = (bconv_ref[:, cols] + wconv_ref[0:1, cols] * v2 + wconv_ref[1:2, cols] * v1
              + wconv_ref[2:3, cols] * v)
        mix_ref[:, :, D_SSM + j * GATE_CHUNK:D_SSM + (j + 1) * GATE_CHUNK] = (
            (gate_b * yc).astype(bf16).reshape(nseq, tt, GATE_CHUNK))
        hs = advance(j, 3, hs, a1, a2)

        for k in range(SLABS):
            hout_ref[j * SLABS + k] = hs[k]
        h_all = jnp.concatenate(
            [bu_ref[j % 2, k, :, 0:half, :].reshape(m2, LANES) for k in range(SLABS)]
            + [bu_ref[j % 2, k, :, half:2 * half, :].reshape(m2, LANES) for k in range(SLABS)],
            axis=1).astype(bf16)
        y2 = jnp.dot(h_all, cc_ref[j], preferred_element_type=f32)
        ucols = slice(j * BLOCK_IN, (j + 1) * BLOCK_IN)
        u_even = perm_ref[0, :, ucols]
        u_odd = perm_ref[1, :, ucols]
        feed = jnp.dot(u_even.astype(bf16), k0_ref[j], preferred_element_type=f32)
        perm_ref[0, :, ucols] = feed + d_ref[:, ucols] * u_even
        perm_ref[1, :, ucols] = y2[:, 0:BLOCK_IN] + d_ref[:, ucols] * u_odd
        for hb in range(2):
            zc = y2[:, BLOCK_IN + hb * CH_BLOCK:BLOCK_IN + (hb + 1) * CH_BLOCK]
            for th2 in range(nth // 2):
                for pr in range(nseq // 2):
                    r0 = ((2 * th2) * 2 + pr) * SUBLANES
                    r1 = ((2 * th2 + 1) * 2 + pr) * SUBLANES
                    for odd_seq in (False, True):
                        start = (2 * pr + odd_seq) * ttp + 2 * SUBLANES * th2 + 2
                        natz_ref[2 * j + hb, pl.ds(start, SUBLANES, stride=2), :] = halves(
                            zc[r0:r0 + SUBLANES], zc[r1:r1 + SUBLANES], odd_seq)

    tail = vbuf_ref[:, tt:tt + SUBLANES, :]
    vbuf_ref[:, 0:SUBLANES, :] = tail
    cout_ref[...] = tail

    for c in range(N_CH_BLOCKS):
        ccols = slice(c * CH_BLOCK, (c + 1) * CH_BLOCK)
        for parity in range(2):
            for th2 in range(nth // 2):
                for pr in range(nseq // 2):
                    r0 = ((2 * th2) * 2 + pr) * SUBLANES
                    r1 = ((2 * th2 + 1) * 2 + pr) * SUBLANES
                    first = perm_ref[parity, r0:r0 + SUBLANES, ccols]
                    second = perm_ref[parity, r1:r1 + SUBLANES, ccols]
                    nat_ref[c, strided(c, 2 * pr, th2, parity), :] = halves(first, second, False)
                    nat_ref[c, strided(c, 2 * pr + 1, th2, parity), :] = halves(first, second, True)
    for hb in range(2):
        rows = slice(hb * m2, (hb + 1) * m2)
        y = jnp.concatenate([nat_ref[c, rows, :] for c in range(N_CH_BLOCKS)], axis=1)
        y = y + jnp.concatenate(
            [jnp.concatenate([natz_ref[c, b * ttp:b * ttp + tt, :] for b in (2 * hb, 2 * hb + 1)], axis=0)
             for c in range(N_CH_BLOCKS)], axis=1)
        y = 0.5 * y * (1.0 + lax.erf(y * INV_SQRT2))
        gl = jnp.dot(y.astype(bf16), wglu_ref[...], preferred_element_type=f32) + bglu_ref[...]
        y_ssm = y * jax.nn.sigmoid(gl)
        mix_ref[2 * hb:2 * hb + 2, :, 0:D_SSM] = y_ssm.astype(bf16).reshape(nseq // 2, tt, D_SSM)
    for c in range(N_CH_BLOCKS):
        for b in range(nseq):
            natz_ref[c, pl.ds(b * ttp, 1), :] = natz_ref[c, pl.ds(b * ttp + tt, 1), :]


def _const_spec(shape):
    zeros = (0,) * len(shape)
    return pl.BlockSpec(shape, lambda *_: zeros, pipeline_mode=pl.Buffered(1))


def _mixer(x, h0, c0, norm_g, w_in, a1_tab, a2_tab, wb, cc, k0, d, w_glu, b_glu, w_conv, b_conv, *, tt):
    nbg, nseq, seqlen, _ = x.shape
    assert nseq == SEQ_PER_STEP and seqlen % tt == 0 and tt % (8 * SUBLANES) == 0
    nt = seqlen // tt
    m = nseq * tt
    f32 = jnp.float32
    nstate_rows = N_BLOCKS * SLABS
    in_specs = [
        pl.BlockSpec((None, nseq, tt, D_MODEL), lambda bg, ti: (bg, 0, ti, 0)),
        pl.BlockSpec((None, nstate_rows, SUBLANES, LANES), lambda bg, ti: (bg, 0, 0, 0)),
        pl.BlockSpec((None, nseq, SUBLANES, D_CONV), lambda bg, ti: (bg, 0, 0, 0)),
        _const_spec((1, D_MODEL)),
        _const_spec((D_MODEL, D_IN)),
        _const_spec((nstate_rows, SUBLANES, LANES)),
        _const_spec((nstate_rows, SUBLANES, LANES)),
        _const_spec((N_CH_BLOCKS, 2 * CH_BLOCK, 2 * GROUPS_PER_CH_BLOCK * N_STATE)),
        _const_spec((N_BLOCKS, 2 * BLOCK_STATE, 2 * BLOCK_IN)),
        _const_spec((N_BLOCKS, BLOCK_IN, BLOCK_IN)),
        _const_spec((1, D_SSM)),
        _const_spec((D_SSM, D_SSM)),
        _const_spec((1, D_SSM)),
        _const_spec((CONV_WIDTH, D_CONV)),
        _const_spec((1, D_CONV)),
    ]
    out_specs = [
        pl.BlockSpec((None, nseq, tt, D_MODEL), lambda bg, ti: (bg, 0, ti, 0)),
        pl.BlockSpec((None, nstate_rows, SUBLANES, LANES), lambda bg, ti: (bg, 0, 0, 0)),
        pl.BlockSpec((None, nseq, SUBLANES, D_CONV), lambda bg, ti: (bg, 0, 0, 0)),
    ]
    out_shape = (
        jax.ShapeDtypeStruct((nbg, nseq, seqlen, D_MODEL), jnp.bfloat16),
        jax.ShapeDtypeStruct((nbg, nstate_rows, SUBLANES, LANES), f32),
        jax.ShapeDtypeStruct((nbg, nseq, SUBLANES, D_CONV), f32),
    )
    scratch = [
        pltpu.VMEM((2, SLABS, tt // SUBLANES, QUAD_ROWS, LANES), f32),
        pltpu.VMEM((2, m // 2, D_SSM), f32),
        pltpu.VMEM((N_CH_BLOCKS, m, CH_BLOCK), f32),
        pltpu.VMEM((N_CH_BLOCKS, nseq * (tt + 2 * SUBLANES), CH_BLOCK), f32),
        pltpu.VMEM((nseq, tt + SUBLANES, D_CONV), f32),
        pltpu.VMEM((m, D_MODEL), jnp.bfloat16),
    ]
    return pl.pallas_call(
        functools.partial(_mixer_kernel, tt=tt),
        grid=(nbg, nt),
        in_specs=in_specs,
        out_specs=out_specs,
        out_shape=out_shape,
        scratch_shapes=scratch,
        compiler_params=pltpu.CompilerParams(
            dimension_semantics=("arbitrary", "arbitrary"), vmem_limit_bytes=VMEM_LIMIT),
        name="mixer",
    )(x, h0, c0, norm_g, w_in, a1_tab, a2_tab, wb, cc, k0, d, w_glu, b_glu, w_conv, b_conv)


def _outproj_kernel(x_ref, mix_ref, wout_ref, g_ref, x1_ref, xn_ref):
    f32 = jnp.float32
    x1 = x_ref[...] + jnp.dot(mix_ref[...], wout_ref[...], preferred_element_type=f32)
    x1_ref[...] = x1
    r = lax.rsqrt(jnp.mean(x1 * x1, axis=-1, keepdims=True) + EPS)
    xn_ref[...] = ((x1 * r) * g_ref[...]).astype(jnp.bfloat16)


def _outproj(x, mix, w_out, norm_g, *, tm):
    tokens = x.shape[0]
    assert tokens % tm == 0
    row_spec = pl.BlockSpec((tm, D_MODEL), lambda i: (i, 0))
    return pl.pallas_call(
        _outproj_kernel,
        grid=(tokens // tm,),
        in_specs=[row_spec, row_spec, _const_spec((D_MODEL, D_MODEL)), _const_spec((1, D_MODEL))],
        out_specs=[row_spec, row_spec],
        out_shape=(jax.ShapeDtypeStruct((tokens, D_MODEL), jnp.float32),
                   jax.ShapeDtypeStruct((tokens, D_MODEL), jnp.bfloat16)),
        compiler_params=pltpu.CompilerParams(
            dimension_semantics=("arbitrary",), vmem_limit_bytes=VMEM_LIMIT),
        name="outproj",
    )(x, mix, w_out, norm_g)


def _mlp_kernel(x1_ref, xn_ref, wup_ref, wdown_ref, gfin_ref, o_ref):
    f32, bf16 = jnp.float32, jnp.bfloat16
    f = pl.program_id(1)
    tf = wup_ref.shape[1]
    acc = jnp.where(f == 0, x1_ref[...], o_ref[...])
    for hcol in range(tf // MLP_DOT_F):
        cols = slice(hcol * MLP_DOT_F, (hcol + 1) * MLP_DOT_F)
        hm = jnp.dot(xn_ref[...], wup_ref[:, cols], preferred_element_type=f32)
        act = jnp.square(jnp.maximum(hm, 0.0)).astype(bf16)
        acc = acc + jnp.dot(act, wdown_ref[cols, :], preferred_element_type=f32)
    o_ref[...] = acc

    @pl.when(f == pl.num_programs(1) - 1)
    def _():
        x2 = o_ref[...]
        r = lax.rsqrt(jnp.mean(x2 * x2, axis=-1, keepdims=True) + EPS)
        o_ref[...] = (x2 * r) * gfin_ref[...]


def _mlp(x1, xn, w_up, w_down, norm_final_g, *, tm):
    tokens = x1.shape[0]
    tf = w_up.shape[2]
    assert tokens % tm == 0 and w_up.shape == (D_FF // tf, D_MODEL, tf) and tf % MLP_DOT_F == 0
    in_specs = [
        pl.BlockSpec((tm, D_MODEL), lambda i, f: (i, 0)),
        pl.BlockSpec((tm, D_MODEL), lambda i, f: (i, 0)),
        pl.BlockSpec((None, D_MODEL, tf), lambda i, f: (f, 0, 0)),
        pl.BlockSpec((tf, D_MODEL), lambda i, f: (f, 0)),
        _const_spec((1, D_MODEL)),
    ]
    return pl.pallas_call(
        _mlp_kernel,
        grid=(tokens // tm, D_FF // tf),
        in_specs=in_specs,
        out_specs=pl.BlockSpec((tm, D_MODEL), lambda i, f: (i, 0)),
        out_shape=jax.ShapeDtypeStruct((tokens, D_MODEL), jnp.float32),
        compiler_params=pltpu.CompilerParams(
            dimension_semantics=("arbitrary", "arbitrary"), vmem_limit_bytes=VMEM_LIMIT),
        name="mlp",
    )(x1, xn, w_up, w_down, norm_final_g)


def _mlp_cast_kernel(x_ref, mix_ref, wout_ref, g_ref, wup_ref, wdown_ref, gfin_ref,
                     o_ref, wup_bf_ref, wdown_bf_ref, xn_ref):
    f32, bf16 = jnp.float32, jnp.bfloat16
    f = pl.program_id(0)
    wup = wup_ref[...].astype(bf16)
    wdown = wdown_ref[...].astype(bf16)
    wup_bf_ref[...] = wup
    wdown_bf_ref[...] = wdown

    @pl.when(f == 0)
    def _():
        x1 = x_ref[...] + jnp.dot(mix_ref[...], wout_ref[...], preferred_element_type=f32)
        o_ref[...] = x1
        r = lax.rsqrt(jnp.mean(x1 * x1, axis=-1, keepdims=True) + EPS)
        xn_ref[...] = ((x1 * r) * g_ref[...]).astype(bf16)

    hm = jnp.dot(xn_ref[...], wup, preferred_element_type=f32)
    act = jnp.square(jnp.maximum(hm, 0.0)).astype(bf16)
    o_ref[...] += jnp.dot(act, wdown, preferred_element_type=f32)

    @pl.when(f == pl.num_programs(0) - 1)
    def _():
        x2 = o_ref[...]
        r = lax.rsqrt(jnp.mean(x2 * x2, axis=-1, keepdims=True) + EPS)
        o_ref[...] = (x2 * r) * gfin_ref[...]


def _mlp_cast(x, mix, w_out, norm_g, w_up_f32, w_down_f32, norm_final_g, *, tf, tf_out):
    tokens = x.shape[0]
    assert D_FF % tf_out == 0 and tf_out % tf == 0
    per_tile = tf_out // tf
    bf16 = jnp.bfloat16
    in_specs = [
        _const_spec((tokens, D_MODEL)),
        _const_spec((tokens, D_MODEL)),
        _const_spec((D_MODEL, D_MODEL)),
        _const_spec((1, D_MODEL)),
        pl.BlockSpec((D_MODEL, tf), lambda f: (0, f)),
        pl.BlockSpec((tf, D_MODEL), lambda f: (f, 0)),
        _const_spec((1, D_MODEL)),
    ]
    out_specs = [
        pl.BlockSpec((tokens, D_MODEL), lambda f: (0, 0)),
        pl.BlockSpec((None, D_MODEL, tf), lambda f: (f // per_tile, 0, f % per_tile)),
        pl.BlockSpec((tf, D_MODEL), lambda f: (f, 0)),
    ]
    out_shape = (
        jax.ShapeDtypeStruct((tokens, D_MODEL), jnp.float32),
        jax.ShapeDtypeStruct((D_FF // tf_out, D_MODEL, tf_out), bf16),
        jax.ShapeDtypeStruct((D_FF, D_MODEL), bf16),
    )
    return pl.pallas_call(
        _mlp_cast_kernel,
        grid=(D_FF // tf,),
        in_specs=in_specs,
        out_specs=out_specs,
        out_shape=out_shape,
        scratch_shapes=[pltpu.VMEM((tokens, D_MODEL), bf16)],
        compiler_params=pltpu.CompilerParams(
            dimension_semantics=("arbitrary",), vmem_limit_bytes=VMEM_LIMIT),
        name="mlp_cast",
    )(x, mix, w_out, norm_g, w_up_f32, w_down_f32, norm_final_g)


def _state_to_rows(h_re, h_im):
    def one(h):
        nb = h.shape[0]
        h = h.reshape(nb // SEQ_PER_STEP, SEQ_PER_STEP, N_BLOCKS, SLABS, LANES)
        return jnp.transpose(h, (0, 2, 3, 1, 4))
    both = jnp.concatenate([one(h_re), one(h_im)], axis=3)
    return both.reshape(both.shape[0], N_BLOCKS * SLABS, 2 * SEQ_PER_STEP, LANES)


def _rows_to_state(rows):
    nbg = rows.shape[0]
    r = rows.reshape(nbg, N_BLOCKS, SLABS, 2, SEQ_PER_STEP, LANES)
    r = jnp.transpose(r, (3, 0, 4, 1, 2, 5))
    r = r.reshape(2, nbg * SEQ_PER_STEP, N_GROUPS, N_STATE)
    return r[0], r[1]


def _a_tables(a_re, a_im, neg_a_im):
    def rows(t, n):
        return jnp.broadcast_to(t.reshape(N_BLOCKS * SLABS, 1, LANES), (N_BLOCKS * SLABS, n, LANES))
    a1 = rows(a_re, 2 * SEQ_PER_STEP)
    a2 = jnp.concatenate([rows(neg_a_im, SEQ_PER_STEP), rows(a_im, SEQ_PER_STEP)], axis=1)
    return a1, a2


def _run_mixer(x, h_re, h_im, conv_prev, lw, tabs, *, tt):
    bsz, seqlen, _ = x.shape
    nbg = bsz // SEQ_PER_STEP
    x4 = x.reshape(nbg, SEQ_PER_STEP, seqlen, D_MODEL)
    h0 = _state_to_rows(h_re, h_im)
    c0 = jnp.pad(conv_prev, ((0, 0), (SUBLANES - (CONV_WIDTH - 1), 0), (0, 0)))
    c0 = c0.reshape(nbg, SEQ_PER_STEP, SUBLANES, D_CONV)
    mix, hout, cout = _mixer(x4, h0, c0, lw["norm_mix_g"], lw["w_in"], *tabs, lw["d"],
                             lw["w_glu"], lw["b_glu"], lw["w_conv"], lw["b_conv"], tt=tt)
    new_re, new_im = _rows_to_state(hout)
    new_conv = cout.reshape(bsz, SUBLANES, D_CONV)[:, SUBLANES - (CONV_WIDTH - 1):, :]
    return mix.reshape(bsz * seqlen, D_MODEL), new_re, new_im, new_conv


def kernel(x_prompt, x_sample, state_ssm_re, state_ssm_im, cache_conv, norm_mix_g, w_in, ssm_lambda_re, ssm_lambda_im, ssm_log_step, ssm_b_re, ssm_b_im, ssm_c_re, ssm_c_im, ssm_d, w_glu, b_glu, w_conv, b_conv, w_out, norm_mlp_g, w_up, w_down, norm_final_g):
    depth = w_in.shape[0]
    assert depth == 1, "the final norm is fused into the last (only) layer"
    bf16 = jnp.bfloat16
    bp = x_prompt.shape[0]
    a_re, a_im, neg_a_im, wb, cc, k0 = _discretise(
        ssm_lambda_re[0], ssm_lambda_im[0], ssm_log_step[0], ssm_b_re[0], ssm_b_im[0], ssm_c_re[0], ssm_c_im[0])
    tabs = _a_tables(a_re, a_im, neg_a_im) + (wb, cc, k0)
    lw = dict(
        norm_mix_g=norm_mix_g[0].reshape(1, D_MODEL), w_in=w_in[0].astype(bf16),
        d=ssm_d[0].reshape(1, D_SSM), w_glu=w_glu[0].astype(bf16), b_glu=b_glu[0].reshape(1, D_SSM),
        w_conv=w_conv[0], b_conv=b_conv[0].reshape(1, D_CONV), w_out=w_out[0].astype(bf16),
        norm_mlp_g=norm_mlp_g[0].reshape(1, D_MODEL))
    gfin = norm_final_g.reshape(1, D_MODEL)

    zeros_h = jnp.zeros((bp, N_GROUPS, N_STATE), jnp.float32)
    zeros_c = jnp.zeros((bp, CONV_WIDTH - 1, D_CONV), x_prompt.dtype)
    mix_s, sre, sim, scv = _run_mixer(x_sample, state_ssm_re[0], state_ssm_im[0], cache_conv[0], lw, tabs, tt=SAMPLE_TILE_T)
    mix_p, pre, pim, pcv = _run_mixer(x_prompt, zeros_h, zeros_h, zeros_c, lw, tabs, tt=PROMPT_TILE_T)
    ys, w_up_bf, w_down_bf = _mlp_cast(x_sample.reshape(-1, D_MODEL), mix_s, lw["w_out"], lw["norm_mlp_g"],
                                       w_up.reshape(D_MODEL, D_FF), w_down.reshape(D_FF, D_MODEL), gfin,
                                       tf=CAST_TILE_F, tf_out=MLP_TILE_F)
    x1_p, xn_p = _outproj(x_prompt.reshape(-1, D_MODEL), mix_p, lw["w_out"], lw["norm_mlp_g"], tm=MLP_TILE_M)
    yp = _mlp(x1_p, xn_p, w_up_bf, w_down_bf, gfin, tm=MLP_TILE_M)
    return (yp.reshape(x_prompt.shape), ys.reshape(x_sample.shape), pre[None], pim[None], pcv[None],
            sre[None], sim[None], scv[None])
```

```python
import functools

import jax
import jax.numpy as jnp
from jax import lax
from jax.experimental import pallas as pl
from jax.experimental.pallas import tpu as pltpu

D_MODEL = 2048
D_SSM = 1024
D_CONV = 1024
N_GROUPS = 64
GROUP_W = 16
N_STATE = 64
D_FF = 8192
D_IN = D_SSM + 3 * D_CONV
CONV_WIDTH = 3
EPS = 1e-6
INV_SQRT2 = 0.7071067811865476

SUBLANES = 8
LANES = 128
SEQ_PER_STEP = 4
GROUPS_PER_BLOCK = 16
N_BLOCKS = N_GROUPS // GROUPS_PER_BLOCK
BLOCK_IN = GROUPS_PER_BLOCK * GROUP_W
BLOCK_STATE = GROUPS_PER_BLOCK * N_STATE
SLABS = BLOCK_STATE // LANES
GATE_CHUNK = 256
QUAD = 4
QUAD_ROWS = 2 * SEQ_PER_STEP * QUAD
CH_BLOCK = LANES
N_CH_BLOCKS = D_SSM // CH_BLOCK
GROUPS_PER_CH_BLOCK = CH_BLOCK // GROUP_W
VMEM_LIMIT = 60 * 1024 * 1024
PROMPT_TILE_T = 128
SAMPLE_TILE_T = 64
MLP_TILE_M = 512
MLP_TILE_F = 2048
MLP_DOT_F = 1024
OUTPROJ_TILE_M = 1024
OUTPROJ_ROW_SPLIT = 4
CAST_TILE_F = 512


def _discretise_kernel(lre_ref, lim_ref, ls_ref, lret_ref, limt_ref, lst_ref, brt_ref, bit_ref, cre_ref, cim_ref,
                       crt_ref, cit_ref,
                       a2re_ref, a2im_ref, na2im_ref, wb_out_ref, cc_out_ref, k0_out_ref,
                       zr_ref, zi_ref, are_ref, aim_ref, art_ref, ait_ref, wb_ref, cc_ref, k0_ref):
    f32 = jnp.float32
    lre = lre_ref[...]
    lim = lim_ref[...]
    delta = jnp.exp(ls_ref[...])
    mag = jnp.exp(lre * delta)
    a_re = mag * jnp.cos(lim * delta)
    a_im = mag * jnp.sin(lim * delta)
    den = lre * lre + lim * lim
    zr_ref[...] = ((a_re - 1.0) * lre + a_im * lim) / den
    zi_ref[...] = (a_im * lre - (a_re - 1.0) * lim) / den
    are_ref[...] = a_re
    aim_ref[...] = a_im
    a2_im = 2.0 * a_re * a_im
    a2re_ref[...] = a_re * a_re - a_im * a_im
    a2im_ref[...] = a2_im
    na2im_ref[...] = -a2_im
    delta_t = jnp.exp(lst_ref[...])
    mag_t = jnp.exp(lret_ref[...] * delta_t)
    art_ref[...] = mag_t * jnp.cos(limt_ref[...] * delta_t)
    ait_ref[...] = mag_t * jnp.sin(limt_ref[...] * delta_t)
    wb_ref[...] = jnp.zeros(wb_ref.shape, f32)
    cc_ref[...] = jnp.zeros(cc_ref.shape, f32)
    k0_ref[...] = jnp.zeros(k0_ref.shape, f32)
    nt_dims = (((1,), (1,)), ((), ()))
    for g in range(N_GROUPS):
        j, gl = divmod(g, GROUPS_PER_BLOCK)
        c, g8 = divmod(g, GROUPS_PER_CH_BLOCK)
        zr = zr_ref[pl.ds(g, 1), :]
        zi = zi_ref[pl.ds(g, 1), :]
        ar = are_ref[pl.ds(g, 1), :]
        ai = aim_ref[pl.ds(g, 1), :]
        br = brt_ref[g]
        bi = bit_ref[g]
        bbr = zr * br - zi * bi
        bbi = zr * bi + zi * br
        rows = slice(g8 * GROUP_W, (g8 + 1) * GROUP_W)
        rows_odd = slice(CH_BLOCK + g8 * GROUP_W, CH_BLOCK + (g8 + 1) * GROUP_W)
        cols = slice(g8 * N_STATE, (g8 + 1) * N_STATE)
        half_w = GROUPS_PER_CH_BLOCK * N_STATE
        cols_im = slice(half_w + g8 * N_STATE, half_w + (g8 + 1) * N_STATE)
        wb_ref[c, rows, cols] = ar * bbr - ai * bbi
        wb_ref[c, rows, cols_im] = ar * bbi + ai * bbr
        wb_ref[c, rows_odd, cols] = bbr
        wb_ref[c, rows_odd, cols_im] = bbi

        cr = cre_ref[g]
        ci = cim_ref[g]
        k0t = (lax.dot_general(bbr, cr, nt_dims, precision=lax.Precision.HIGHEST, preferred_element_type=f32)
               - lax.dot_general(bbi, ci, nt_dims, precision=lax.Precision.HIGHEST, preferred_element_type=f32))
        grows = slice(gl * GROUP_W, (gl + 1) * GROUP_W)
        k0_ref[j, grows, grows] = k0t

        arc = art_ref[:, g:g + 1]
        aic = ait_ref[:, g:g + 1]
        crt = crt_ref[g]
        cit = cit_ref[g]
        srows = slice(gl * N_STATE, (gl + 1) * N_STATE)
        srows_im = slice(BLOCK_STATE + gl * N_STATE, BLOCK_STATE + (gl + 1) * N_STATE)
        ecols = slice(BLOCK_IN + gl * GROUP_W, BLOCK_IN + (gl + 1) * GROUP_W)
        cc_ref[j, srows, grows] = crt
        cc_ref[j, srows_im, grows] = -cit
        cc_ref[j, srows, ecols] = crt * arc - cit * aic
        cc_ref[j, srows_im, ecols] = -(cit * arc + crt * aic)
    wb_out_ref[...] = wb_ref[...].astype(jnp.bfloat16)
    cc_out_ref[...] = cc_ref[...].astype(jnp.bfloat16)
    k0_out_ref[...] = k0_ref[...].astype(jnp.bfloat16)


def _discretise(lam_re, lam_im, log_step, b_re, b_im, c_re, c_im):
    f32, bf16 = jnp.float32, jnp.bfloat16
    brt = jnp.transpose(b_re, (0, 2, 1))
    bit = jnp.transpose(b_im, (0, 2, 1))
    crt = jnp.transpose(c_re, (0, 2, 1))
    cit = jnp.transpose(c_im, (0, 2, 1))
    gp = (N_GROUPS, N_STATE)
    wb_shape = (N_CH_BLOCKS, 2 * CH_BLOCK, 2 * GROUPS_PER_CH_BLOCK * N_STATE)
    cc_shape = (N_BLOCKS, 2 * BLOCK_STATE, 2 * BLOCK_IN)
    k0_shape = (N_BLOCKS, BLOCK_IN, BLOCK_IN)
    out_shape = (
        jax.ShapeDtypeStruct(gp, f32), jax.ShapeDtypeStruct(gp, f32), jax.ShapeDtypeStruct(gp, f32),
        jax.ShapeDtypeStruct(wb_shape, bf16), jax.ShapeDtypeStruct(cc_shape, bf16),
        jax.ShapeDtypeStruct(k0_shape, bf16),
    )
    scratch = [pltpu.VMEM(gp, f32) for _ in range(4)] + [pltpu.VMEM((N_STATE, N_GROUPS), f32) for _ in range(2)] + [
        pltpu.VMEM(wb_shape, f32), pltpu.VMEM(cc_shape, f32), pltpu.VMEM(k0_shape, f32)]
    return pl.pallas_call(
        _discretise_kernel,
        out_shape=out_shape,
        scratch_shapes=scratch,
        compiler_params=pltpu.CompilerParams(vmem_limit_bytes=VMEM_LIMIT),
        name="discretise",
    )(lam_re, lam_im, log_step.reshape(N_GROUPS, 1), lam_re.T, lam_im.T, log_step.reshape(1, N_GROUPS),
      brt, bit, c_re, c_im, crt, cit)


def _mixer_kernel(x_ref, h0_ref, c0_ref, g_ref, win_ref, a1_ref, a2_ref, wb_ref, cc_ref, k0_ref, d_ref,
                  wglu_ref, bglu_ref, wconv_ref, bconv_ref,
                  mix_ref, hout_ref, cout_ref,
                  bu_ref, perm_ref, nat_ref, natz_ref, vbuf_ref, xn_ref, *, tt):
    f32, bf16 = jnp.float32, jnp.bfloat16
    nseq = SEQ_PER_STEP
    m = nseq * tt
    m2 = m // 2
    nth = tt // SUBLANES
    half = nseq * QUAD
    ttp = tt + 2 * SUBLANES
    ti = pl.program_id(1)

    @pl.when(ti == 0)
    def _():
        hout_ref[...] = h0_ref[...]
        vbuf_ref[:, 0:SUBLANES, :] = c0_ref[...]
        natz_ref[...] = jnp.zeros(natz_ref.shape, f32)
        for j in range(N_BLOCKS):
            slabs = [h0_ref[j * SLABS + k] for k in range(SLABS)]
            lhs = jnp.concatenate(slabs + [pltpu.roll(sl, nseq, axis=0) for sl in slabs], axis=1).astype(bf16)
            z0 = jnp.dot(lhs, cc_ref[j, :, BLOCK_IN:2 * BLOCK_IN], preferred_element_type=f32)
            for hb in range(2):
                for b in range(nseq):
                    natz_ref[2 * j + hb, pl.ds(b * ttp, 1), :] = z0[b:b + 1, hb * CH_BLOCK:(hb + 1) * CH_BLOCK]

    for hb in range(2):
        rows = slice(hb * m2, (hb + 1) * m2)
        x = x_ref[2 * hb:2 * hb + 2].reshape(m2, D_MODEL)
        r = lax.rsqrt(jnp.mean(x * x, axis=-1, keepdims=True) + EPS)
        xn_ref[rows, :] = ((x * r) * g_ref[...]).astype(bf16)
        u = jnp.dot(xn_ref[rows, :], win_ref[:, 0:D_SSM], preferred_element_type=f32)
        for c in range(N_CH_BLOCKS):
            nat_ref[c, rows, :] = u[:, c * CH_BLOCK:(c + 1) * CH_BLOCK]

    low_rows = lax.broadcasted_iota(jnp.int32, (SUBLANES, LANES), 0) < QUAD

    def halves(first, second, high):
        if high:
            return jnp.where(low_rows, pltpu.roll(first, QUAD, axis=0), second)
        return jnp.where(low_rows, first, pltpu.roll(second, QUAD, axis=0))

    def strided(c, b, th2, parity):
        return pl.ds(b * tt + 2 * SUBLANES * th2 + parity, SUBLANES, stride=2)

    def gather(c, parity):
        out = [None] * (2 * nth)
        for th2 in range(nth // 2):
            for pr in range(nseq // 2):
                e0 = nat_ref[c, strided(c, 2 * pr, th2, parity), :]
                e1 = nat_ref[c, strided(c, 2 * pr + 1, th2, parity), :]
                out[(2 * th2) * 2 + pr] = halves(e0, e1, False)
                out[(2 * th2 + 1) * 2 + pr] = halves(e0, e1, True)
        return jnp.concatenate(out, axis=0)

    sections = 4
    pairs = tt // 2 // sections
    per_gate = D_CONV // GATE_CHUNK
    assert per_gate == N_BLOCKS

    def gate_cols(gate, c):
        start = D_SSM + gate * D_CONV + c * GATE_CHUNK
        return slice(start, start + GATE_CHUNK)

    def project_b(j):
        half_slabs = SLABS // 2
        half_w = GROUPS_PER_CH_BLOCK * N_STATE
        for hb in range(2):
            c = 2 * j + hb
            ccols = slice(c * CH_BLOCK, (c + 1) * CH_BLOCK)
            ev = gather(c, 0)
            od = gather(c, 1)
            perm_ref[0, :, ccols] = ev
            perm_ref[1, :, ccols] = od
            lhs = jnp.concatenate([ev, od], axis=1).astype(bf16)
            v = jnp.dot(lhs, wb_ref[c], preferred_element_type=f32)
            for kk in range(half_slabs):
                k = hb * half_slabs + kk
                re = v[:, kk * LANES:(kk + 1) * LANES]
                im = v[:, half_w + kk * LANES:half_w + (kk + 1) * LANES]
                bu_ref[j % 2, k, :, 0:half, :] = re.reshape(nth, half, LANES)
                bu_ref[j % 2, k, :, half:2 * half, :] = im.reshape(nth, half, LANES)

    def advance(j, q, hs, a1, a2):
        buf = j % 2
        for p in range(q * pairs, (q + 1) * pairs):
            th, p4 = divmod(p, QUAD)
            rows = pl.ds(p4, 2 * nseq, stride=QUAD)
            for k in range(SLABS):
                h = hs[k]
                nh = a1[k] * h + a2[k] * pltpu.roll(h, nseq, axis=0) + bu_ref[buf, k, th, rows, :]
                bu_ref[buf, k, th, rows, :] = nh
                hs[k] = nh
        return hs

    project_b(0)
    for j in range(N_BLOCKS):
        if j + 1 < N_BLOCKS:
            project_b(j + 1)
        a1 = [a1_ref[j * SLABS + k] for k in range(SLABS)]
        a2 = [a2_ref[j * SLABS + k] for k in range(SLABS)]
        hs = [hout_ref[j * SLABS + k] for k in range(SLABS)]

        cols = slice(j * GATE_CHUNK, (j + 1) * GATE_CHUNK)
        gate_b = jnp.dot(xn_ref[...], win_ref[:, gate_cols(0, j)], preferred_element_type=f32)
        hs = advance(j, 0, hs, a1, a2)
        gate_c = jnp.dot(xn_ref[...], win_ref[:, gate_cols(1, j)], preferred_element_type=f32)
        hs = advance(j, 1, hs, a1, a2)
        xv = jnp.dot(xn_ref[...], win_ref[:, gate_cols(2, j)], preferred_element_type=f32)
        hs = advance(j, 2, hs, a1, a2)
        v = gate_c * xv
        vbuf_ref[:, SUBLANES:, cols] = v.reshape(nseq, tt, GATE_CHUNK)
        v1 = vbuf_ref[:, SUBLANES - 1:SUBLANES - 1 + tt, cols].reshape(m, GATE_CHUNK)
        v2 = vbuf_ref[:, SUBLANES - 2:SUBLANES - 2 + tt, cols].reshape(m, GATE_CHUNK)
        yc = (bconv_ref[:, cols] + wconv_ref[0:1, cols] * v2 + wconv_ref[1:2, cols] * v1
              + wconv_ref[2:3, cols] * v)
        mix_ref[:, :, D_SSM + j * GATE_CHUNK:D_SSM + (j + 1) * GATE_CHUNK] = (
            (gate_b * yc).astype(bf16).reshape(nseq, tt, GATE_CHUNK))
        hs = advance(j, 3, hs, a1, a2)

        for k in range(SLABS):
            hout_ref[j * SLABS + k] = hs[k]
        h_all = jnp.concatenate(
            [bu_ref[j % 2, k, :, 0:half, :].reshape(m2, LANES) for k in range(SLABS)]
            + [bu_ref[j % 2, k, :, half:2 * half, :].reshape(m2, LANES) for k in range(SLABS)],
            axis=1).astype(bf16)
        y2 = jnp.dot(h_all, cc_ref[j], preferred_element_type=f32)
        ucols = slice(j * BLOCK_IN, (j + 1) * BLOCK_IN)
        u_even = perm_ref[0, :, ucols]
        u_odd = perm_ref[1, :, ucols]
        feed = jnp.dot(u_even.astype(bf16), k0_ref[j], preferred_element_type=f32)
        perm_ref[0, :, ucols] = feed + d_ref[:, ucols] * u_even
        perm_ref[1, :, ucols] = y2[:, 0:BLOCK_IN] + d_ref[:, ucols] * u_odd
        for hb in range(2):
            zc = y2[:, BLOCK_IN + hb * CH_BLOCK:BLOCK_IN + (hb + 1) * CH_BLOCK]
            for th2 in range(nth // 2):
                for pr in range(nseq // 2):
                    r0 = ((2 * th2) * 2 + pr) * SUBLANES
                    r1 = ((2 * th2 + 1) * 2 + pr) * SUBLANES
                    for odd_seq in (False, True):
                        start = (2 * pr + odd_seq) * ttp + 2 * SUBLANES * th2 + 2
                        natz_ref[2 * j + hb, pl.ds(start, SUBLANES, stride=2), :] = halves(
                            zc[r0:r0 + SUBLANES], zc[r1:r1 + SUBLANES], odd_seq)

    tail = vbuf_ref[:, tt:tt + SUBLANES, :]
    vbuf_ref[:, 0:SUBLANES, :] = tail
    cout_ref[...] = tail

    for c in range(N_CH_BLOCKS):
        ccols = slice(c * CH_BLOCK, (c + 1) * CH_BLOCK)
        for parity in range(2):
            for th2 in range(nth // 2):
                for pr in range(nseq // 2):
                    r0 = ((2 * th2) * 2 + pr) * SUBLANES
                    r1 = ((2 * th2 + 1) * 2 + pr) * SUBLANES
                    first = perm_ref[parity, r0:r0 + SUBLANES, ccols]
                    second = perm_ref[parity, r1:r1 + SUBLANES, ccols]
                    nat_ref[c, strided(c, 2 * pr, th2, parity), :] = halves(first, second, False)
                    nat_ref[c, strided(c, 2 * pr + 1, th2, parity), :] = halves(first, second, True)
    for hb in range(2):
        rows = slice(hb * m2, (hb + 1) * m2)
        y = jnp.concatenate([nat_ref[c, rows, :] for c in range(N_CH_BLOCKS)], axis=1)
        y = y + jnp.concatenate(
            [jnp.concatenate([natz_ref[c, b * ttp:b * ttp + tt, :] for b in (2 * hb, 2 * hb + 1)], axis=0)
             for c in range(N_CH_BLOCKS)], axis=1)
        y = 0.5 * y * (1.0 + lax.erf(y * INV_SQRT2))
        gl = jnp.dot(y.astype(bf16), wglu_ref[...], preferred_element_type=f32) + bglu_ref[...]
        y_ssm = y * jax.nn.sigmoid(gl)
        mix_ref[2 * hb:2 * hb + 2, :, 0:D_SSM] = y_ssm.astype(bf16).reshape(nseq // 2, tt, D_SSM)
    for c in range(N_CH_BLOCKS):
        for b in range(nseq):
            natz_ref[c, pl.ds(b * ttp, 1), :] = natz_ref[c, pl.ds(b * ttp + tt, 1), :]


def _const_spec(shape):
    zeros = (0,) * len(shape)
    return pl.BlockSpec(shape, lambda *_: zeros, pipeline_mode=pl.Buffered(1))


def _mixer(x, h0, c0, norm_g, w_in, a1_tab, a2_tab, wb, cc, k0, d, w_glu, b_glu, w_conv, b_conv, *, tt):
    nbg, nseq, seqlen, _ = x.shape
    assert nseq == SEQ_PER_STEP and seqlen % tt == 0 and tt % (8 * SUBLANES) == 0
    nt = seqlen // tt
    m = nseq * tt
    f32 = jnp.float32
    nstate_rows = N_BLOCKS * SLABS
    in_specs = [
        pl.BlockSpec((None, nseq, tt, D_MODEL), lambda bg, ti: (bg, 0, ti, 0)),
        pl.BlockSpec((None, nstate_rows, SUBLANES, LANES), lambda bg, ti: (bg, 0, 0, 0)),
        pl.BlockSpec((None, nseq, SUBLANES, D_CONV), lambda bg, ti: (bg, 0, 0, 0)),
        _const_spec((1, D_MODEL)),
        _const_spec((D_MODEL, D_IN)),
        _const_spec((nstate_rows, SUBLANES, LANES)),
        _const_spec((nstate_rows, SUBLANES, LANES)),
        _const_spec((N_CH_BLOCKS, 2 * CH_BLOCK, 2 * GROUPS_PER_CH_BLOCK * N_STATE)),
        _const_spec((N_BLOCKS, 2 * BLOCK_STATE, 2 * BLOCK_IN)),
        _const_spec((N_BLOCKS, BLOCK_IN, BLOCK_IN)),
        _const_spec((1, D_SSM)),
        _const_spec((D_SSM, D_SSM)),
        _const_spec((1, D_SSM)),
        _const_spec((CONV_WIDTH, D_CONV)),
        _const_spec((1, D_CONV)),
    ]
    out_specs = [
        pl.BlockSpec((None, nseq, tt, D_MODEL), lambda bg, ti: (bg, 0, ti, 0)),
        pl.BlockSpec((None, nstate_rows, SUBLANES, LANES), lambda bg, ti: (bg, 0, 0, 0)),
        pl.BlockSpec((None, nseq, SUBLANES, D_CONV), lambda bg, ti: (bg, 0, 0, 0)),
    ]
    out_shape = (
        jax.ShapeDtypeStruct((nbg, nseq, seqlen, D_MODEL), jnp.bfloat16),
        jax.ShapeDtypeStruct((nbg, nstate_rows, SUBLANES, LANES), f32),
        jax.ShapeDtypeStruct((nbg, nseq, SUBLANES, D_CONV), f32),
    )
    scratch = [
        pltpu.VMEM((2, SLABS, tt // SUBLANES, QUAD_ROWS, LANES), f32),
        pltpu.VMEM((2, m // 2, D_SSM), f32),
        pltpu.VMEM((N_CH_BLOCKS, m, CH_BLOCK), f32),
        pltpu.VMEM((N_CH_BLOCKS, nseq * (tt + 2 * SUBLANES), CH_BLOCK), f32),
        pltpu.VMEM((nseq, tt + SUBLANES, D_CONV), f32),
        pltpu.VMEM((m, D_MODEL), jnp.bfloat16),
    ]
    return pl.pallas_call(
        functools.partial(_mixer_kernel, tt=tt),
        grid=(nbg, nt),
        in_specs=in_specs,
        out_specs=out_specs,
        out_shape=out_shape,
        scratch_shapes=scratch,
        compiler_params=pltpu.CompilerParams(
            dimension_semantics=("arbitrary", "arbitrary"), vmem_limit_bytes=VMEM_LIMIT),
        name="mixer",
    )(x, h0, c0, norm_g, w_in, a1_tab, a2_tab, wb, cc, k0, d, w_glu, b_glu, w_conv, b_conv)


def _outproj_kernel(x_ref, mix_ref, wout_ref, g_ref, x1_ref, xn_ref):
    f32 = jnp.float32
    part = x_ref.shape[0] // OUTPROJ_ROW_SPLIT
    for h in range(OUTPROJ_ROW_SPLIT):
        rows = slice(h * part, (h + 1) * part)
        x1 = x_ref[rows, :] + jnp.dot(mix_ref[rows, :], wout_ref[...], preferred_element_type=f32)
        x1_ref[rows, :] = x1
        r = lax.rsqrt(jnp.mean(x1 * x1, axis=-1, keepdims=True) + EPS)
        xn_ref[rows, :] = ((x1 * r) * g_ref[...]).astype(jnp.bfloat16)


def _outproj(x, mix, w_out, norm_g, *, tm):
    tokens = x.shape[0]
    assert tokens % tm == 0
    row_spec = pl.BlockSpec((tm, D_MODEL), lambda i: (i, 0))
    return pl.pallas_call(
        _outproj_kernel,
        grid=(tokens // tm,),
        in_specs=[row_spec, row_spec, _const_spec((D_MODEL, D_MODEL)), _const_spec((1, D_MODEL))],
        out_specs=[row_spec, row_spec],
        out_shape=(jax.ShapeDtypeStruct((tokens, D_MODEL), jnp.float32),
                   jax.ShapeDtypeStruct((tokens, D_MODEL), jnp.bfloat16)),
        compiler_params=pltpu.CompilerParams(
            dimension_semantics=("arbitrary",), vmem_limit_bytes=VMEM_LIMIT),
        name="outproj",
    )(x, mix, w_out, norm_g)


def _mlp_kernel(x1_ref, xn_ref, wup_ref, wdown_ref, gfin_ref, o_ref):
    f32, bf16 = jnp.float32, jnp.bfloat16
    f = pl.program_id(1)
    tf = wup_ref.shape[1]
    acc = jnp.where(f == 0, x1_ref[...], o_ref[...])
    for hcol in range(tf // MLP_DOT_F):
        cols = slice(hcol * MLP_DOT_F, (hcol + 1) * MLP_DOT_F)
        hm = jnp.dot(xn_ref[...], wup_ref[:, cols], preferred_element_type=f32)
        act = jnp.square(jnp.maximum(hm, 0.0)).astype(bf16)
        acc = acc + jnp.dot(act, wdown_ref[cols, :], preferred_element_type=f32)
    o_ref[...] = acc

    @pl.when(f == pl.num_programs(1) - 1)
    def _():
        x2 = o_ref[...]
        r = lax.rsqrt(jnp.mean(x2 * x2, axis=-1, keepdims=True) + EPS)
        o_ref[...] = (x2 * r) * gfin_ref[...]


def _mlp(x1, xn, w_up, w_down, norm_final_g, *, tm):
    tokens = x1.shape[0]
    tf = w_up.shape[2]
    assert tokens % tm == 0 and w_up.shape == (D_FF // tf, D_MODEL, tf) and tf % MLP_DOT_F == 0
    in_specs = [
        pl.BlockSpec((tm, D_MODEL), lambda i, f: (i, 0)),
        pl.BlockSpec((tm, D_MODEL), lambda i, f: (i, 0)),
        pl.BlockSpec((None, D_MODEL, tf), lambda i, f: (f, 0, 0)),
        pl.BlockSpec((tf, D_MODEL), lambda i, f: (f, 0)),
        _const_spec((1, D_MODEL)),
    ]
    return pl.pallas_call(
        _mlp_kernel,
        grid=(tokens // tm, D_FF // tf),
        in_specs=in_specs,
        out_specs=pl.BlockSpec((tm, D_MODEL), lambda i, f: (i, 0)),
        out_shape=jax.ShapeDtypeStruct((tokens, D_MODEL), jnp.float32),
        compiler_params=pltpu.CompilerParams(
            dimension_semantics=("arbitrary", "arbitrary"), vmem_limit_bytes=VMEM_LIMIT),
        name="mlp",
    )(x1, xn, w_up, w_down, norm_final_g)


def _mlp_cast_kernel(x_ref, mix_ref, wout_ref, g_ref, wup_ref, wdown_ref, gfin_ref,
                     o_ref, wup_bf_ref, wdown_bf_ref, xn_ref):
    f32, bf16 = jnp.float32, jnp.bfloat16
    f = pl.program_id(0)
    wup = wup_ref[...].astype(bf16)
    wdown = wdown_ref[...].astype(bf16)
    wup_bf_ref[...] = wup
    wdown_bf_ref[...] = wdown

    @pl.when(f == 0)
    def _():
        x1 = x_ref[...] + jnp.dot(mix_ref[...], wout_ref[...], preferred_element_type=f32)
        o_ref[...] = x1
        r = lax.rsqrt(jnp.mean(x1 * x1, axis=-1, keepdims=True) + EPS)
        xn_ref[...] = ((x1 * r) * g_ref[...]).astype(bf16)

    hm = jnp.dot(xn_ref[...], wup, preferred_element_type=f32)
    act = jnp.square(jnp.maximum(hm, 0.0)).astype(bf16)
    o_ref[...] += jnp.dot(act, wdown, preferred_element_type=f32)

    @pl.when(f == pl.num_programs(0) - 1)
    def _():
        x2 = o_ref[...]
        r = lax.rsqrt(jnp.mean(x2 * x2, axis=-1, keepdims=True) + EPS)
        o_ref[...] = (x2 * r) * gfin_ref[...]


def _mlp_cast(x, mix, w_out, norm_g, w_up_f32, w_down_f32, norm_final_g, *, tf, tf_out):
    tokens = x.shape[0]
    assert D_FF % tf_out == 0 and tf_out % tf == 0
    per_tile = tf_out // tf
    bf16 = jnp.bfloat16
    in_specs = [
        _const_spec((tokens, D_MODEL)),
        _const_spec((tokens, D_MODEL)),
        _const_spec((D_MODEL, D_MODEL)),
        _const_spec((1, D_MODEL)),
        pl.BlockSpec((D_MODEL, tf), lambda f: (0, f)),
        pl.BlockSpec((tf, D_MODEL), lambda f: (f, 0)),
        _const_spec((1, D_MODEL)),
    ]
    out_specs = [
        pl.BlockSpec((tokens, D_MODEL), lambda f: (0, 0)),
        pl.BlockSpec((None, D_MODEL, tf), lambda f: (f // per_tile, 0, f % per_tile)),
        pl.BlockSpec((tf, D_MODEL), lambda f: (f, 0)),
    ]
    out_shape = (
        jax.ShapeDtypeStruct((tokens, D_MODEL), jnp.float32),
        jax.ShapeDtypeStruct((D_FF // tf_out, D_MODEL, tf_out), bf16),
        jax.ShapeDtypeStruct((D_FF, D_MODEL), bf16),
    )
    return pl.pallas_call(
        _mlp_cast_kernel,
        grid=(D_FF // tf,),
        in_specs=in_specs,
        out_specs=out_specs,
        out_shape=out_shape,
        scratch_shapes=[pltpu.VMEM((tokens, D_MODEL), bf16)],
        compiler_params=pltpu.CompilerParams(
            dimension_semantics=("arbitrary",), vmem_limit_bytes=VMEM_LIMIT),
        name="mlp_cast",
    )(x, mix, w_out, norm_g, w_up_f32, w_down_f32, norm_final_g)


def _state_to_rows(h_re, h_im):
    def one(h):
        nb = h.shape[0]
        h = h.reshape(nb // SEQ_PER_STEP, SEQ_PER_STEP, N_BLOCKS, SLABS, LANES)
        return jnp.transpose(h, (0, 2, 3, 1, 4))
    both = jnp.concatenate([one(h_re), one(h_im)], axis=3)
    return both.reshape(both.shape[0], N_BLOCKS * SLABS, 2 * SEQ_PER_STEP, LANES)


def _rows_to_state(rows):
    nbg = rows.shape[0]
    r = rows.reshape(nbg, N_BLOCKS, SLABS, 2, SEQ_PER_STEP, LANES)
    r = jnp.transpose(r, (3, 0, 4, 1, 2, 5))
    r = r.reshape(2, nbg * SEQ_PER_STEP, N_GROUPS, N_STATE)
    return r[0], r[1]


def _a_tables(a_re, a_im, neg_a_im):
    def rows(t, n):
        return jnp.broadcast_to(t.reshape(N_BLOCKS * SLABS, 1, LANES), (N_BLOCKS * SLABS, n, LANES))
    a1 = rows(a_re, 2 * SEQ_PER_STEP)
    a2 = jnp.concatenate([rows(neg_a_im, SEQ_PER_STEP), rows(a_im, SEQ_PER_STEP)], axis=1)
    return a1, a2


def _run_mixer(x, h_re, h_im, conv_prev, lw, tabs, *, tt):
    bsz, seqlen, _ = x.shape
    nbg = bsz // SEQ_PER_STEP
    x4 = x.reshape(nbg, SEQ_PER_STEP, seqlen, D_MODEL)
    h0 = _state_to_rows(h_re, h_im)
    c0 = jnp.pad(conv_prev, ((0, 0), (SUBLANES - (CONV_WIDTH - 1), 0), (0, 0)))
    c0 = c0.reshape(nbg, SEQ_PER_STEP, SUBLANES, D_CONV)
    mix, hout, cout = _mixer(x4, h0, c0, lw["norm_mix_g"], lw["w_in"], *tabs, lw["d"],
                             lw["w_glu"], lw["b_glu"], lw["w_conv"], lw["b_conv"], tt=tt)
    new_re, new_im = _rows_to_state(hout)
    new_conv = cout.reshape(bsz, SUBLANES, D_CONV)[:, SUBLANES - (CONV_WIDTH - 1):, :]
    return mix.reshape(bsz * seqlen, D_MODEL), new_re, new_im, new_conv


def kernel(x_prompt, x_sample, state_ssm_re, state_ssm_im, cache_conv, norm_mix_g, w_in, ssm_lambda_re, ssm_lambda_im, ssm_log_step, ssm_b_re, ssm_b_im, ssm_c_re, ssm_c_im, ssm_d, w_glu, b_glu, w_conv, b_conv, w_out, norm_mlp_g, w_up, w_down, norm_final_g):
    depth = w_in.shape[0]
    assert depth == 1, "the final norm is fused into the last (only) layer"
    bf16 = jnp.bfloat16
    bp = x_prompt.shape[0]
    a_re, a_im, neg_a_im, wb, cc, k0 = _discretise(
        ssm_lambda_re[0], ssm_lambda_im[0], ssm_log_step[0], ssm_b_re[0], ssm_b_im[0], ssm_c_re[0], ssm_c_im[0])
    tabs = _a_tables(a_re, a_im, neg_a_im) + (wb, cc, k0)
    lw = dict(
        norm_mix_g=norm_mix_g[0].reshape(1, D_MODEL), w_in=w_in[0].astype(bf16),
        d=ssm_d[0].reshape(1, D_SSM), w_glu=w_glu[0].astype(bf16), b_glu=b_glu[0].reshape(1, D_SSM),
        w_conv=w_conv[0], b_conv=b_conv[0].reshape(1, D_CONV), w_out=w_out[0].astype(bf16),
        norm_mlp_g=norm_mlp_g[0].reshape(1, D_MODEL))
    gfin = norm_final_g.reshape(1, D_MODEL)

    zeros_h = jnp.zeros((bp, N_GROUPS, N_STATE), jnp.float32)
    zeros_c = jnp.zeros((bp, CONV_WIDTH - 1, D_CONV), x_prompt.dtype)
    mix_s, sre, sim, scv = _run_mixer(x_sample, state_ssm_re[0], state_ssm_im[0], cache_conv[0], lw, tabs, tt=SAMPLE_TILE_T)
    mix_p, pre, pim, pcv = _run_mixer(x_prompt, zeros_h, zeros_h, zeros_c, lw, tabs, tt=PROMPT_TILE_T)
    ys, w_up_bf, w_down_bf = _mlp_cast(x_sample.reshape(-1, D_MODEL), mix_s, lw["w_out"], lw["norm_mlp_g"],
                                       w_up.reshape(D_MODEL, D_FF), w_down.reshape(D_FF, D_MODEL), gfin,
                                       tf=CAST_TILE_F, tf_out=MLP_TILE_F)
    x1_p, xn_p = _outproj(x_prompt.reshape(-1, D_MODEL), mix_p, lw["w_out"], lw["norm_mlp_g"], tm=OUTPROJ_TILE_M)
    yp = _mlp(x1_p, xn_p, w_up_bf, w_down_bf, gfin, tm=MLP_TILE_M)
    return (yp.reshape(x_prompt.shape), ys.reshape(x_sample.shape), pre[None], pim[None], pcv[None],
            sre[None], sim[None], scv[None])
```

```python
import functools

import jax
import jax.numpy as jnp
from jax import lax
from jax.experimental import pallas as pl
from jax.experimental.pallas import tpu as pltpu

D_MODEL = 2048
D_SSM = 1024
D_CONV = 1024
N_GROUPS = 64
GROUP_W = 16
N_STATE = 64
D_FF = 8192
D_IN = D_SSM + 3 * D_CONV
CONV_WIDTH = 3
EPS = 1e-6
INV_SQRT2 = 0.7071067811865476

SUBLANES = 8
LANES = 128
SEQ_PER_STEP = 4
GROUPS_PER_BLOCK = 16
N_BLOCKS = N_GROUPS // GROUPS_PER_BLOCK
BLOCK_IN = GROUPS_PER_BLOCK * GROUP_W
BLOCK_STATE = GROUPS_PER_BLOCK * N_STATE
SLABS = BLOCK_STATE // LANES
GATE_CHUNK = 256
QUAD = 4
QUAD_ROWS = 2 * SEQ_PER_STEP * QUAD
CH_BLOCK = LANES
N_CH_BLOCKS = D_SSM // CH_BLOCK
GROUPS_PER_CH_BLOCK = CH_BLOCK // GROUP_W
VMEM_LIMIT = 60 * 1024 * 1024
PROMPT_TILE_T = 128
SAMPLE_TILE_T = 64
MLP_TILE_M = 512
MLP_TILE_F = 2048
MLP_DOT_F = 1024
CAST_TILE_F = 512


def _discretise_kernel(lre_ref, lim_ref, ls_ref, lret_ref, limt_ref, lst_ref, brt_ref, bit_ref, cre_ref, cim_ref,
                       crt_ref, cit_ref,
                       a2re_ref, a2im_ref, na2im_ref, wb_out_ref, cc_out_ref, k0_out_ref,
                       zr_ref, zi_ref, are_ref, aim_ref, art_ref, ait_ref, wb_ref, cc_ref, k0_ref):
    f32 = jnp.float32
    lre = lre_ref[...]
    lim = lim_ref[...]
    delta = jnp.exp(ls_ref[...])
    mag = jnp.exp(lre * delta)
    a_re = mag * jnp.cos(lim * delta)
    a_im = mag * jnp.sin(lim * delta)
    den = lre * lre + lim * lim
    zr_ref[...] = ((a_re - 1.0) * lre + a_im * lim) / den
    zi_ref[...] = (a_im * lre - (a_re - 1.0) * lim) / den
    are_ref[...] = a_re
    aim_ref[...] = a_im
    a2_im = 2.0 * a_re * a_im
    a2re_ref[...] = a_re * a_re - a_im * a_im
    a2im_ref[...] = a2_im
    na2im_ref[...] = -a2_im
    delta_t = jnp.exp(lst_ref[...])
    mag_t = jnp.exp(lret_ref[...] * delta_t)
    art_ref[...] = mag_t * jnp.cos(limt_ref[...] * delta_t)
    ait_ref[...] = mag_t * jnp.sin(limt_ref[...] * delta_t)
    wb_ref[...] = jnp.zeros(wb_ref.shape, f32)
    cc_ref[...] = jnp.zeros(cc_ref.shape, f32)
    k0_ref[...] = jnp.zeros(k0_ref.shape, f32)
    nt_dims = (((1,), (1,)), ((), ()))
    for g in range(N_GROUPS):
        j, gl = divmod(g, GROUPS_PER_BLOCK)
        c, g8 = divmod(g, GROUPS_PER_CH_BLOCK)
        zr = zr_ref[pl.ds(g, 1), :]
        zi = zi_ref[pl.ds(g, 1), :]
        ar = are_ref[pl.ds(g, 1), :]
        ai = aim_ref[pl.ds(g, 1), :]
        br = brt_ref[g]
        bi = bit_ref[g]
        bbr = zr * br - zi * bi
        bbi = zr * bi + zi * br
        rows = slice(g8 * GROUP_W, (g8 + 1) * GROUP_W)
        rows_odd = slice(CH_BLOCK + g8 * GROUP_W, CH_BLOCK + (g8 + 1) * GROUP_W)
        cols = slice(g8 * N_STATE, (g8 + 1) * N_STATE)
        half_w = GROUPS_PER_CH_BLOCK * N_STATE
        cols_im = slice(half_w + g8 * N_STATE, half_w + (g8 + 1) * N_STATE)
        wb_ref[c, rows, cols] = ar * bbr - ai * bbi
        wb_ref[c, rows, cols_im] = ar * bbi + ai * bbr
        wb_ref[c, rows_odd, cols] = bbr
        wb_ref[c, rows_odd, cols_im] = bbi

        cr = cre_ref[g]
        ci = cim_ref[g]
        k0t = (lax.dot_general(bbr, cr, nt_dims, precision=lax.Precision.HIGHEST, preferred_element_type=f32)
               - lax.dot_general(bbi, ci, nt_dims, precision=lax.Precision.HIGHEST, preferred_element_type=f32))
        grows = slice(gl * GROUP_W, (gl + 1) * GROUP_W)
        k0_ref[j, grows, grows] = k0t

        arc = art_ref[:, g:g + 1]
        aic = ait_ref[:, g:g + 1]
        crt = crt_ref[g]
        cit = cit_ref[g]
        srows = slice(gl * N_STATE, (gl + 1) * N_STATE)
        srows_im = slice(BLOCK_STATE + gl * N_STATE, BLOCK_STATE + (gl + 1) * N_STATE)
        ecols = slice(BLOCK_IN + gl * GROUP_W, BLOCK_IN + (gl + 1) * GROUP_W)
        cc_ref[j, srows, grows] = crt
        cc_ref[j, srows_im, grows] = -cit
        cc_ref[j, srows, ecols] = crt * arc - cit * aic
        cc_ref[j, srows_im, ecols] = -(cit * arc + crt * aic)
    wb_out_ref[...] = wb_ref[...].astype(jnp.bfloat16)
    cc_out_ref[...] = cc_ref[...].astype(jnp.bfloat16)
    k0_out_ref[...] = k0_ref[...].astype(jnp.bfloat16)


def _discretise(lam_re, lam_im, log_step, b_re, b_im, c_re, c_im):
    f32, bf16 = jnp.float32, jnp.bfloat16
    brt = jnp.transpose(b_re, (0, 2, 1))
    bit = jnp.transpose(b_im, (0, 2, 1))
    crt = jnp.transpose(c_re, (0, 2, 1))
    cit = jnp.transpose(c_im, (0, 2, 1))
    gp = (N_GROUPS, N_STATE)
    wb_shape = (N_CH_BLOCKS, 2 * CH_BLOCK, 2 * GROUPS_PER_CH_BLOCK * N_STATE)
    cc_shape = (N_BLOCKS, 2 * BLOCK_STATE, 2 * BLOCK_IN)
    k0_shape = (N_BLOCKS, BLOCK_IN, BLOCK_IN)
    out_shape = (
        jax.ShapeDtypeStruct(gp, f32), jax.ShapeDtypeStruct(gp, f32), jax.ShapeDtypeStruct(gp, f32),
        jax.ShapeDtypeStruct(wb_shape, bf16), jax.ShapeDtypeStruct(cc_shape, bf16),
        jax.ShapeDtypeStruct(k0_shape, bf16),
    )
    scratch = [pltpu.VMEM(gp, f32) for _ in range(4)] + [pltpu.VMEM((N_STATE, N_GROUPS), f32) for _ in range(2)] + [
        pltpu.VMEM(wb_shape, f32), pltpu.VMEM(cc_shape, f32), pltpu.VMEM(k0_shape, f32)]
    return pl.pallas_call(
        _discretise_kernel,
        out_shape=out_shape,
        scratch_shapes=scratch,
        compiler_params=pltpu.CompilerParams(vmem_limit_bytes=VMEM_LIMIT),
        name="discretise",
    )(lam_re, lam_im, log_step.reshape(N_GROUPS, 1), lam_re.T, lam_im.T, log_step.reshape(1, N_GROUPS),
      brt, bit, c_re, c_im, crt, cit)


def _mixer_kernel(x_ref, h0_ref, c0_ref, g_ref, win_ref, a1_ref, a2_ref, wb_ref, cc_ref, k0_ref, d_ref,
                  wglu_ref, bglu_ref, wconv_ref, bconv_ref,
                  mix_ref, hout_ref, cout_ref,
                  bu_ref, perm_ref, nat_ref, natz_ref, vbuf_ref, xn_ref, *, tt):
    f32, bf16 = jnp.float32, jnp.bfloat16
    nseq = SEQ_PER_STEP
    m = nseq * tt
    m2 = m // 2
    nth = tt // SUBLANES
    half = nseq * QUAD
    ttp = tt + 2 * SUBLANES
    ti = pl.program_id(1)

    @pl.when(ti == 0)
    def _():
        hout_ref[...] = h0_ref[...]
        vbuf_ref[:, 0:SUBLANES, :] = c0_ref[...]
        natz_ref[...] = jnp.zeros(natz_ref.shape, f32)
        for j in range(N_BLOCKS):
            slabs = [h0_ref[j * SLABS + k] for k in range(SLABS)]
            lhs = jnp.concatenate(slabs + [pltpu.roll(sl, nseq, axis=0) for sl in slabs], axis=1).astype(bf16)
            z0 = jnp.dot(lhs, cc_ref[j, :, BLOCK_IN:2 * BLOCK_IN], preferred_element_type=f32)
            for hb in range(2):
                for b in range(nseq):
                    natz_ref[2 * j + hb, pl.ds(b * ttp, 1), :] = z0[b:b + 1, hb * CH_BLOCK:(hb + 1) * CH_BLOCK]

    for hb in range(2):
        rows = slice(hb * m2, (hb + 1) * m2)
        x = x_ref[2 * hb:2 * hb + 2].reshape(m2, D_MODEL)
        r = lax.rsqrt(jnp.mean(x * x, axis=-1, keepdims=True) + EPS)
        xn_ref[rows, :] = ((x * r) * g_ref[...]).astype(bf16)
        u = jnp.dot(xn_ref[rows, :], win_ref[:, 0:D_SSM], preferred_element_type=f32)
        for c in range(N_CH_BLOCKS):
            nat_ref[c, rows, :] = u[:, c * CH_BLOCK:(c + 1) * CH_BLOCK]

    low_rows = lax.broadcasted_iota(jnp.int32, (SUBLANES, LANES), 0) < QUAD

    def halves(first, second, high):
        if high:
            return jnp.where(low_rows, pltpu.roll(first, QUAD, axis=0), second)
        return jnp.where(low_rows, first, pltpu.roll(second, QUAD, axis=0))

    def strided(c, b, th2, parity):
        return pl.ds(b * tt + 2 * SUBLANES * th2 + parity, SUBLANES, stride=2)

    def gather(c, parity):
        out = [None] * (2 * nth)
        for th2 in range(nth // 2):
            for pr in range(nseq // 2):
                e0 = nat_ref[c, strided(c, 2 * pr, th2, parity), :]
                e1 = nat_ref[c, strided(c, 2 * pr + 1, th2, parity), :]
                out[(2 * th2) * 2 + pr] = halves(e0, e1, False)
                out[(2 * th2 + 1) * 2 + pr] = halves(e0, e1, True)
        return jnp.concatenate(out, axis=0)

    sections = 4
    pairs = tt // 2 // sections
    per_gate = D_CONV // GATE_CHUNK
    assert per_gate == N_BLOCKS

    def gate_cols(gate, c):
        start = D_SSM + gate * D_CONV + c * GATE_CHUNK
        return slice(start, start + GATE_CHUNK)

    def project_b(j):
        half_slabs = SLABS // 2
        half_w = GROUPS_PER_CH_BLOCK * N_STATE
        for hb in range(2):
            c = 2 * j + hb
            ccols = slice(c * CH_BLOCK, (c + 1) * CH_BLOCK)
            ev = gather(c, 0)
            od = gather(c, 1)
            perm_ref[0, :, ccols] = ev
            perm_ref[1, :, ccols] = od
            lhs = jnp.concatenate([ev, od], axis=1).astype(bf16)
            v = jnp.dot(lhs, wb_ref[c], preferred_element_type=f32)
            for kk in range(half_slabs):
                k = hb * half_slabs + kk
                re = v[:, kk * LANES:(kk + 1) * LANES]
                im = v[:, half_w + kk * LANES:half_w + (kk + 1) * LANES]
                bu_ref[j % 2, k, :, 0:half, :] = re.reshape(nth, half, LANES)
                bu_ref[j % 2, k, :, half:2 * half, :] = im.reshape(nth, half, LANES)

    def advance(j, q, hs, a1, a2):
        buf = j % 2
        for p in range(q * pairs, (q + 1) * pairs):
            th, p4 = divmod(p, QUAD)
            rows = pl.ds(p4, 2 * nseq, stride=QUAD)
            for k in range(SLABS):
                h = hs[k]
                nh = a1[k] * h + a2[k] * pltpu.roll(h, nseq, axis=0) + bu_ref[buf, k, th, rows, :]
                bu_ref[buf, k, th, rows, :] = nh
                hs[k] = nh
        return hs

    project_b(0)
    for j in range(N_BLOCKS):
        if j + 1 < N_BLOCKS:
            project_b(j + 1)
        a1 = [a1_ref[j * SLABS + k] for k in range(SLABS)]
        a2 = [a2_ref[j * SLABS + k] for k in range(SLABS)]
        hs = [hout_ref[j * SLABS + k] for k in range(SLABS)]

        cols = slice(j * GATE_CHUNK, (j + 1) * GATE_CHUNK)
        gate_b = jnp.dot(xn_ref[...], win_ref[:, gate_cols(0, j)], preferred_element_type=f32)
        hs = advance(j, 0, hs, a1, a2)
        gate_c = jnp.dot(xn_ref[...], win_ref[:, gate_cols(1, j)], preferred_element_type=f32)
        hs = advance(j, 1, hs, a1, a2)
        xv = jnp.dot(xn_ref[...], win_ref[:, gate_cols(2, j)], preferred_element_type=f32)
        hs = advance(j, 2, hs, a1, a2)
        v = gate_c * xv
        vbuf_ref[:, SUBLANES:, cols] = v.reshape(nseq, tt, GATE_CHUNK)
        v1 = vbuf_ref[:, SUBLANES - 1:SUBLANES - 1 + tt, cols].reshape(m, GATE_CHUNK)
        v2 = vbuf_ref[:, SUBLANES - 2:SUBLANES - 2 + tt, cols].reshape(m, GATE_CHUNK)
        yc = (bconv_ref[:, cols] + wconv_ref[0:1, cols] * v2 + wconv_ref[1:2, cols] * v1
              + wconv_ref[2:3, cols] * v)
        mix_ref[:, :, D_SSM + j * GATE_CHUNK:D_SSM + (j + 1) * GATE_CHUNK] = (
            (gate_b * yc).astype(bf16).reshape(nseq, tt, GATE_CHUNK))
        hs = advance(j, 3, hs, a1, a2)

        for k in range(SLABS):
            hout_ref[j * SLABS + k] = hs[k]
        h_all = jnp.concatenate(
            [bu_ref[j % 2, k, :, 0:half, :].reshape(m2, LANES) for k in range(SLABS)]
            + [bu_ref[j % 2, k, :, half:2 * half, :].reshape(m2, LANES) for k in range(SLABS)],
            axis=1).astype(bf16)
        y2 = jnp.dot(h_all, cc_ref[j], preferred_element_type=f32)
        ucols = slice(j * BLOCK_IN, (j + 1) * BLOCK_IN)
        u_even = perm_ref[0, :, ucols]
        u_odd = perm_ref[1, :, ucols]
        feed = jnp.dot(u_even.astype(bf16), k0_ref[j], preferred_element_type=f32)
        perm_ref[0, :, ucols] = feed + d_ref[:, ucols] * u_even
        perm_ref[1, :, ucols] = y2[:, 0:BLOCK_IN] + d_ref[:, ucols] * u_odd
        for hb in range(2):
            zc = y2[:, BLOCK_IN + hb * CH_BLOCK:BLOCK_IN + (hb + 1) * CH_BLOCK]
            for th2 in range(nth // 2):
                for pr in range(nseq // 2):
                    r0 = ((2 * th2) * 2 + pr) * SUBLANES
                    r1 = ((2 * th2 + 1) * 2 + pr) * SUBLANES
                    for odd_seq in (False, True):
                        start = (2 * pr + odd_seq) * ttp + 2 * SUBLANES * th2 + 2
                        natz_ref[2 * j + hb, pl.ds(start, SUBLANES, stride=2), :] = halves(
                            zc[r0:r0 + SUBLANES], zc[r1:r1 + SUBLANES], odd_seq)

    tail = vbuf_ref[:, tt:tt + SUBLANES, :]
    vbuf_ref[:, 0:SUBLANES, :] = tail
    cout_ref[...] = tail

    for c in range(N_CH_BLOCKS):
        ccols = slice(c * CH_BLOCK, (c + 1) * CH_BLOCK)
        for parity in range(2):
            for th2 in range(nth // 2):
                for pr in range(nseq // 2):
                    r0 = ((2 * th2) * 2 + pr) * SUBLANES
                    r1 = ((2 * th2 + 1) * 2 + pr) * SUBLANES
                    first = perm_ref[parity, r0:r0 + SUBLANES, ccols]
                    second = perm_ref[parity, r1:r1 + SUBLANES, ccols]
                    nat_ref[c, strided(c, 2 * pr, th2, parity), :] = halves(first, second, False)
                    nat_ref[c, strided(c, 2 * pr + 1, th2, parity), :] = halves(first, second, True)
    for hb in range(2):
        rows = slice(hb * m2, (hb + 1) * m2)
        y = jnp.concatenate([nat_ref[c, rows, :] for c in range(N_CH_BLOCKS)], axis=1)
        y = y + jnp.concatenate(
            [jnp.concatenate([natz_ref[c, b * ttp:b * ttp + tt, :] for b in (2 * hb, 2 * hb + 1)], axis=0)
             for c in range(N_CH_BLOCKS)], axis=1)
        y = 0.5 * y * (1.0 + lax.erf(y * INV_SQRT2))
        gl = jnp.dot(y.astype(bf16), wglu_ref[...], preferred_element_type=f32) + bglu_ref[...]
        y_ssm = y * jax.nn.sigmoid(gl)
        mix_ref[2 * hb:2 * hb + 2, :, 0:D_SSM] = y_ssm.astype(bf16).reshape(nseq // 2, tt, D_SSM)
    for c in range(N_CH_BLOCKS):
        for b in range(nseq):
            natz_ref[c, pl.ds(b * ttp, 1), :] = natz_ref[c, pl.ds(b * ttp + tt, 1), :]


def _const_spec(shape):
    zeros = (0,) * len(shape)
    return pl.BlockSpec(shape, lambda *_: zeros, pipeline_mode=pl.Buffered(1))


def _mixer(x, h0, c0, norm_g, w_in, a1_tab, a2_tab, wb, cc, k0, d, w_glu, b_glu, w_conv, b_conv, *, tt):
    nbg, nseq, seqlen, _ = x.shape
    assert nseq == SEQ_PER_STEP and seqlen % tt == 0 and tt % (8 * SUBLANES) == 0
    nt = seqlen // tt
    m = nseq * tt
    f32 = jnp.float32
    nstate_rows = N_BLOCKS * SLABS
    in_specs = [
        pl.BlockSpec((None, nseq, tt, D_MODEL), lambda bg, ti: (bg, 0, ti, 0)),
        pl.BlockSpec((None, nstate_rows, SUBLANES, LANES), lambda bg, ti: (bg, 0, 0, 0)),
        pl.BlockSpec((None, nseq, SUBLANES, D_CONV), lambda bg, ti: (bg, 0, 0, 0)),
        _const_spec((1, D_MODEL)),
        _const_spec((D_MODEL, D_IN)),
        _const_spec((nstate_rows, SUBLANES, LANES)),
        _const_spec((nstate_rows, SUBLANES, LANES)),
        _const_spec((N_CH_BLOCKS, 2 * CH_BLOCK, 2 * GROUPS_PER_CH_BLOCK * N_STATE)),
        _const_spec((N_BLOCKS, 2 * BLOCK_STATE, 2 * BLOCK_IN)),
        _const_spec((N_BLOCKS, BLOCK_IN, BLOCK_IN)),
        _const_spec((1, D_SSM)),
        _const_spec((D_SSM, D_SSM)),
        _const_spec((1, D_SSM)),
        _const_spec((CONV_WIDTH, D_CONV)),
        _const_spec((1, D_CONV)),
    ]
    out_specs = [
        pl.BlockSpec((None, nseq, tt, D_MODEL), lambda bg, ti: (bg, 0, ti, 0)),
        pl.BlockSpec((None, nstate_rows, SUBLANES, LANES), lambda bg, ti: (bg, 0, 0, 0)),
        pl.BlockSpec((None, nseq, SUBLANES, D_CONV), lambda bg, ti: (bg, 0, 0, 0)),
    ]
    out_shape = (
        jax.ShapeDtypeStruct((nbg, nseq, seqlen, D_MODEL), jnp.bfloat16),
        jax.ShapeDtypeStruct((nbg, nstate_rows, SUBLANES, LANES), f32),
        jax.ShapeDtypeStruct((nbg, nseq, SUBLANES, D_CONV), f32),
    )
    scratch = [
        pltpu.VMEM((2, SLABS, tt // SUBLANES, QUAD_ROWS, LANES), f32),
        pltpu.VMEM((2, m // 2, D_SSM), f32),
        pltpu.VMEM((N_CH_BLOCKS, m, CH_BLOCK), f32),
        pltpu.VMEM((N_CH_BLOCKS, nseq * (tt + 2 * SUBLANES), CH_BLOCK), f32),
        pltpu.VMEM((nseq, tt + SUBLANES, D_CONV), f32),
        pltpu.VMEM((m, D_MODEL), jnp.bfloat16),
    ]
    return pl.pallas_call(
        functools.partial(_mixer_kernel, tt=tt),
        grid=(nbg, nt),
        in_specs=in_specs,
        out_specs=out_specs,
        out_shape=out_shape,
        scratch_shapes=scratch,
        compiler_params=pltpu.CompilerParams(
            dimension_semantics=("arbitrary", "arbitrary"), vmem_limit_bytes=VMEM_LIMIT),
        name="mixer",
    )(x, h0, c0, norm_g, w_in, a1_tab, a2_tab, wb, cc, k0, d, w_glu, b_glu, w_conv, b_conv)


def _ffn_columns(xn, wup_ref, wdown_ref, acc):
    f32 = jnp.float32
    for hcol in range(wup_ref.shape[1] // MLP_DOT_F):
        cols = slice(hcol * MLP_DOT_F, (hcol + 1) * MLP_DOT_F)
        hm = jnp.dot(xn, wup_ref[:, cols], preferred_element_type=f32)
        act = jnp.square(jnp.maximum(hm, 0.0)).astype(jnp.bfloat16)
        acc = acc + jnp.dot(act, wdown_ref[cols, :], preferred_element_type=f32)
    return acc


def _outproj_kernel(x_ref, mix_ref, wout_ref, g_ref, wup_ref, wdown_ref, acc_ref, xn_ref):
    f32 = jnp.float32
    x1 = x_ref[...] + jnp.dot(mix_ref[...], wout_ref[...], preferred_element_type=f32)
    r = lax.rsqrt(jnp.mean(x1 * x1, axis=-1, keepdims=True) + EPS)
    xn = ((x1 * r) * g_ref[...]).astype(jnp.bfloat16)
    xn_ref[...] = xn
    acc_ref[...] = _ffn_columns(xn, wup_ref, wdown_ref, x1)


def _outproj(x, mix, w_out, norm_g, w_up, w_down, *, tm):
    tokens = x.shape[0]
    tf = w_up.shape[2]
    assert tokens % tm == 0
    row_spec = pl.BlockSpec((tm, D_MODEL), lambda i: (i, 0))
    in_specs = [row_spec, row_spec, _const_spec((D_MODEL, D_MODEL)), _const_spec((1, D_MODEL)),
                pl.BlockSpec((None, D_MODEL, tf), lambda i: (0, 0, 0), pipeline_mode=pl.Buffered(1)),
                pl.BlockSpec((tf, D_MODEL), lambda i: (0, 0), pipeline_mode=pl.Buffered(1))]
    return pl.pallas_call(
        _outproj_kernel,
        grid=(tokens // tm,),
        in_specs=in_specs,
        out_specs=[row_spec, row_spec],
        out_shape=(jax.ShapeDtypeStruct((tokens, D_MODEL), jnp.float32),
                   jax.ShapeDtypeStruct((tokens, D_MODEL), jnp.bfloat16)),
        compiler_params=pltpu.CompilerParams(
            dimension_semantics=("arbitrary",), vmem_limit_bytes=VMEM_LIMIT),
        name="outproj",
    )(x, mix, w_out, norm_g, w_up, w_down)


def _mlp_kernel(x1_ref, xn_ref, wup_ref, wdown_ref, gfin_ref, o_ref):
    f = pl.program_id(1)
    acc = jnp.where(f == 0, x1_ref[...], o_ref[...])
    o_ref[...] = _ffn_columns(xn_ref[...], wup_ref, wdown_ref, acc)

    @pl.when(f == pl.num_programs(1) - 1)
    def _():
        x2 = o_ref[...]
        r = lax.rsqrt(jnp.mean(x2 * x2, axis=-1, keepdims=True) + EPS)
        o_ref[...] = (x2 * r) * gfin_ref[...]


def _mlp(x1, xn, w_up, w_down, norm_final_g, *, tm, first_tile):
    tokens = x1.shape[0]
    tf = w_up.shape[2]
    assert tokens % tm == 0 and w_up.shape == (D_FF // tf, D_MODEL, tf) and tf % MLP_DOT_F == 0
    assert 0 <= first_tile < D_FF // tf
    in_specs = [
        pl.BlockSpec((tm, D_MODEL), lambda i, f: (i, 0)),
        pl.BlockSpec((tm, D_MODEL), lambda i, f: (i, 0)),
        pl.BlockSpec((None, D_MODEL, tf), lambda i, f: (f + first_tile, 0, 0)),
        pl.BlockSpec((tf, D_MODEL), lambda i, f: (f + first_tile, 0)),
        _const_spec((1, D_MODEL)),
    ]
    return pl.pallas_call(
        _mlp_kernel,
        grid=(tokens // tm, D_FF // tf - first_tile),
        in_specs=in_specs,
        out_specs=pl.BlockSpec((tm, D_MODEL), lambda i, f: (i, 0)),
        out_shape=jax.ShapeDtypeStruct((tokens, D_MODEL), jnp.float32),
        compiler_params=pltpu.CompilerParams(
            dimension_semantics=("arbitrary", "arbitrary"), vmem_limit_bytes=VMEM_LIMIT),
        name="mlp",
    )(x1, xn, w_up, w_down, norm_final_g)


def _mlp_cast_kernel(x_ref, mix_ref, wout_ref, g_ref, wup_ref, wdown_ref, gfin_ref,
                     o_ref, wup_bf_ref, wdown_bf_ref, xn_ref):
    f32, bf16 = jnp.float32, jnp.bfloat16
    f = pl.program_id(0)
    wup = wup_ref[...].astype(bf16)
    wdown = wdown_ref[...].astype(bf16)
    wup_bf_ref[...] = wup
    wdown_bf_ref[...] = wdown

    @pl.when(f == 0)
    def _():
        x1 = x_ref[...] + jnp.dot(mix_ref[...], wout_ref[...], preferred_element_type=f32)
        o_ref[...] = x1
        r = lax.rsqrt(jnp.mean(x1 * x1, axis=-1, keepdims=True) + EPS)
        xn_ref[...] = ((x1 * r) * g_ref[...]).astype(bf16)

    hm = jnp.dot(xn_ref[...], wup, preferred_element_type=f32)
    act = jnp.square(jnp.maximum(hm, 0.0)).astype(bf16)
    o_ref[...] += jnp.dot(act, wdown, preferred_element_type=f32)

    @pl.when(f == pl.num_programs(0) - 1)
    def _():
        x2 = o_ref[...]
        r = lax.rsqrt(jnp.mean(x2 * x2, axis=-1, keepdims=True) + EPS)
        o_ref[...] = (x2 * r) * gfin_ref[...]


def _mlp_cast(x, mix, w_out, norm_g, w_up_f32, w_down_f32, norm_final_g, *, tf, tf_out):
    tokens = x.shape[0]
    assert D_FF % tf_out == 0 and tf_out % tf == 0
    per_tile = tf_out // tf
    bf16 = jnp.bfloat16
    in_specs = [
        _const_spec((tokens, D_MODEL)),
        _const_spec((tokens, D_MODEL)),
        _const_spec((D_MODEL, D_MODEL)),
        _const_spec((1, D_MODEL)),
        pl.BlockSpec((D_MODEL, tf), lambda f: (0, f)),
        pl.BlockSpec((tf, D_MODEL), lambda f: (f, 0)),
        _const_spec((1, D_MODEL)),
    ]
    out_specs = [
        pl.BlockSpec((tokens, D_MODEL), lambda f: (0, 0)),
        pl.BlockSpec((None, D_MODEL, tf), lambda f: (f // per_tile, 0, f % per_tile)),
        pl.BlockSpec((tf, D_MODEL), lambda f: (f, 0)),
    ]
    out_shape = (
        jax.ShapeDtypeStruct((tokens, D_MODEL), jnp.float32),
        jax.ShapeDtypeStruct((D_FF // tf_out, D_MODEL, tf_out), bf16),
        jax.ShapeDtypeStruct((D_FF, D_MODEL), bf16),
    )
    return pl.pallas_call(
        _mlp_cast_kernel,
        grid=(D_FF // tf,),
        in_specs=in_specs,
        out_specs=out_specs,
        out_shape=out_shape,
        scratch_shapes=[pltpu.VMEM((tokens, D_MODEL), bf16)],
        compiler_params=pltpu.CompilerParams(
            dimension_semantics=("arbitrary",), vmem_limit_bytes=VMEM_LIMIT),
        name="mlp_cast",
    )(x, mix, w_out, norm_g, w_up_f32, w_down_f32, norm_final_g)


def _state_to_rows(h_re, h_im):
    def one(h):
        nb = h.shape[0]
        h = h.reshape(nb // SEQ_PER_STEP, SEQ_PER_STEP, N_BLOCKS, SLABS, LANES)
        return jnp.transpose(h, (0, 2, 3, 1, 4))
    both = jnp.concatenate([one(h_re), one(h_im)], axis=3)
    return both.reshape(both.shape[0], N_BLOCKS * SLABS, 2 * SEQ_PER_STEP, LANES)


def _rows_to_state(rows):
    nbg = rows.shape[0]
    r = rows.reshape(nbg, N_BLOCKS, SLABS, 2, SEQ_PER_STEP, LANES)
    r = jnp.transpose(r, (3, 0, 4, 1, 2, 5))
    r = r.reshape(2, nbg * SEQ_PER_STEP, N_GROUPS, N_STATE)
    return r[0], r[1]


def _a_tables(a_re, a_im, neg_a_im):
    def rows(t, n):
        return jnp.broadcast_to(t.reshape(N_BLOCKS * SLABS, 1, LANES), (N_BLOCKS * SLABS, n, LANES))
    a1 = rows(a_re, 2 * SEQ_PER_STEP)
    a2 = jnp.concatenate([rows(neg_a_im, SEQ_PER_STEP), rows(a_im, SEQ_PER_STEP)], axis=1)
    return a1, a2


def _run_mixer(x, h_re, h_im, conv_prev, lw, tabs, *, tt):
    bsz, seqlen, _ = x.shape
    nbg = bsz // SEQ_PER_STEP
    x4 = x.reshape(nbg, SEQ_PER_STEP, seqlen, D_MODEL)
    h0 = _state_to_rows(h_re, h_im)
    c0 = jnp.pad(conv_prev, ((0, 0), (SUBLANES - (CONV_WIDTH - 1), 0), (0, 0)))
    c0 = c0.reshape(nbg, SEQ_PER_STEP, SUBLANES, D_CONV)
    mix, hout, cout = _mixer(x4, h0, c0, lw["norm_mix_g"], lw["w_in"], *tabs, lw["d"],
                             lw["w_glu"], lw["b_glu"], lw["w_conv"], lw["b_conv"], tt=tt)
    new_re, new_im = _rows_to_state(hout)
    new_conv = cout.reshape(bsz, SUBLANES, D_CONV)[:, SUBLANES - (CONV_WIDTH - 1):, :]
    return mix.reshape(bsz * seqlen, D_MODEL), new_re, new_im, new_conv


def kernel(x_prompt, x_sample, state_ssm_re, state_ssm_im, cache_conv, norm_mix_g, w_in, ssm_lambda_re, ssm_lambda_im, ssm_log_step, ssm_b_re, ssm_b_im, ssm_c_re, ssm_c_im, ssm_d, w_glu, b_glu, w_conv, b_conv, w_out, norm_mlp_g, w_up, w_down, norm_final_g):
    depth = w_in.shape[0]
    assert depth == 1, "the final norm is fused into the last (only) layer"
    bf16 = jnp.bfloat16
    bp = x_prompt.shape[0]
    a_re, a_im, neg_a_im, wb, cc, k0 = _discretise(
        ssm_lambda_re[0], ssm_lambda_im[0], ssm_log_step[0], ssm_b_re[0], ssm_b_im[0], ssm_c_re[0], ssm_c_im[0])
    tabs = _a_tables(a_re, a_im, neg_a_im) + (wb, cc, k0)
    lw = dict(
        norm_mix_g=norm_mix_g[0].reshape(1, D_MODEL), w_in=w_in[0].astype(bf16),
        d=ssm_d[0].reshape(1, D_SSM), w_glu=w_glu[0].astype(bf16), b_glu=b_glu[0].reshape(1, D_SSM),
        w_conv=w_conv[0], b_conv=b_conv[0].reshape(1, D_CONV), w_out=w_out[0].astype(bf16),
        norm_mlp_g=norm_mlp_g[0].reshape(1, D_MODEL))
    gfin = norm_final_g.reshape(1, D_MODEL)

    zeros_h = jnp.zeros((bp, N_GROUPS, N_STATE), jnp.float32)
    zeros_c = jnp.zeros((bp, CONV_WIDTH - 1, D_CONV), x_prompt.dtype)
    mix_s, sre, sim, scv = _run_mixer(x_sample, state_ssm_re[0], state_ssm_im[0], cache_conv[0], lw, tabs, tt=SAMPLE_TILE_T)
    mix_p, pre, pim, pcv = _run_mixer(x_prompt, zeros_h, zeros_h, zeros_c, lw, tabs, tt=PROMPT_TILE_T)
    ys, w_up_bf, w_down_bf = _mlp_cast(x_sample.reshape(-1, D_MODEL), mix_s, lw["w_out"], lw["norm_mlp_g"],
                                       w_up.reshape(D_MODEL, D_FF), w_down.reshape(D_FF, D_MODEL), gfin,
                                       tf=CAST_TILE_F, tf_out=MLP_TILE_F)
    acc_p, xn_p = _outproj(x_prompt.reshape(-1, D_MODEL), mix_p, lw["w_out"], lw["norm_mlp_g"], w_up_bf, w_down_bf,
                           tm=MLP_TILE_M)
    yp = _mlp(acc_p, xn_p, w_up_bf, w_down_bf, gfin, tm=MLP_TILE_M, first_tile=1)
    return (yp.reshape(x_prompt.shape), ys.reshape(x_sample.shape), pre[None], pim[None], pcv[None],
            sre[None], sim[None], scv[None])
```

```python
import functools

import jax
import jax.numpy as jnp
from jax import lax
from jax.experimental import pallas as pl
from jax.experimental.pallas import tpu as pltpu

D_MODEL = 2048
D_SSM = 1024
D_CONV = 1024
N_GROUPS = 64
GROUP_W = 16
N_STATE = 64
D_FF = 8192
D_IN = D_SSM + 3 * D_CONV
CONV_WIDTH = 3
EPS = 1e-6
INV_SQRT2 = 0.7071067811865476

SUBLANES = 8
LANES = 128
SEQ_PER_STEP = 4
GROUPS_PER_BLOCK = 16
N_BLOCKS = N_GROUPS // GROUPS_PER_BLOCK
BLOCK_IN = GROUPS_PER_BLOCK * GROUP_W
BLOCK_STATE = GROUPS_PER_BLOCK * N_STATE
SLABS = BLOCK_STATE // LANES
GATE_CHUNK = 256
QUAD = 4
QUAD_ROWS = 2 * SEQ_PER_STEP * QUAD
CH_BLOCK = LANES
N_CH_BLOCKS = D_SSM // CH_BLOCK
GROUPS_PER_CH_BLOCK = CH_BLOCK // GROUP_W
VMEM_LIMIT = 60 * 1024 * 1024
PROMPT_TILE_T = 128
SAMPLE_TILE_T = 64
MLP_TILE_M = 512
MLP_TILE_F = 2048
MLP_DOT_F = 1024
CAST_TILE_F = 256
CAST_RING = 4


def _discretise_kernel(lre_ref, lim_ref, ls_ref, lret_ref, limt_ref, lst_ref, brt_ref, bit_ref, cre_ref, cim_ref,
                       crt_ref, cit_ref,
                       a2re_ref, a2im_ref, na2im_ref, wb_out_ref, cc_out_ref, k0_out_ref,
                       zr_ref, zi_ref, are_ref, aim_ref, art_ref, ait_ref, wb_ref, cc_ref, k0_ref):
    f32 = jnp.float32
    lre = lre_ref[...]
    lim = lim_ref[...]
    delta = jnp.exp(ls_ref[...])
    mag = jnp.exp(lre * delta)
    a_re = mag * jnp.cos(lim * delta)
    a_im = mag * jnp.sin(lim * delta)
    den = lre * lre + lim * lim
    zr_ref[...] = ((a_re - 1.0) * lre + a_im * lim) / den
    zi_ref[...] = (a_im * lre - (a_re - 1.0) * lim) / den
    are_ref[...] = a_re
    aim_ref[...] = a_im
    a2_im = 2.0 * a_re * a_im
    a2re_ref[...] = a_re * a_re - a_im * a_im
    a2im_ref[...] = a2_im
    na2im_ref[...] = -a2_im
    delta_t = jnp.exp(lst_ref[...])
    mag_t = jnp.exp(lret_ref[...] * delta_t)
    art_ref[...] = mag_t * jnp.cos(limt_ref[...] * delta_t)
    ait_ref[...] = mag_t * jnp.sin(limt_ref[...] * delta_t)
    wb_ref[...] = jnp.zeros(wb_ref.shape, f32)
    cc_ref[...] = jnp.zeros(cc_ref.shape, f32)
    k0_ref[...] = jnp.zeros(k0_ref.shape, f32)
    nt_dims = (((1,), (1,)), ((), ()))
    for g in range(N_GROUPS):
        j, gl = divmod(g, GROUPS_PER_BLOCK)
        c, g8 = divmod(g, GROUPS_PER_CH_BLOCK)
        zr = zr_ref[pl.ds(g, 1), :]
        zi = zi_ref[pl.ds(g, 1), :]
        ar = are_ref[pl.ds(g, 1), :]
        ai = aim_ref[pl.ds(g, 1), :]
        br = brt_ref[g]
        bi = bit_ref[g]
        bbr = zr * br - zi * bi
        bbi = zr * bi + zi * br
        rows = slice(g8 * GROUP_W, (g8 + 1) * GROUP_W)
        rows_odd = slice(CH_BLOCK + g8 * GROUP_W, CH_BLOCK + (g8 + 1) * GROUP_W)
        cols = slice(g8 * N_STATE, (g8 + 1) * N_STATE)
        half_w = GROUPS_PER_CH_BLOCK * N_STATE
        cols_im = slice(half_w + g8 * N_STATE, half_w + (g8 + 1) * N_STATE)
        wb_ref[c, rows, cols] = ar * bbr - ai * bbi
        wb_ref[c, rows, cols_im] = ar * bbi + ai * bbr
        wb_ref[c, rows_odd, cols] = bbr
        wb_ref[c, rows_odd, cols_im] = bbi

        cr = cre_ref[g]
        ci = cim_ref[g]
        k0t = (lax.dot_general(bbr, cr, nt_dims, precision=lax.Precision.HIGHEST, preferred_element_type=f32)
               - lax.dot_general(bbi, ci, nt_dims, precision=lax.Precision.HIGHEST, preferred_element_type=f32))
        grows = slice(gl * GROUP_W, (gl + 1) * GROUP_W)
        k0_ref[j, grows, grows] = k0t

        arc = art_ref[:, g:g + 1]
        aic = ait_ref[:, g:g + 1]
        crt = crt_ref[g]
        cit = cit_ref[g]
        srows = slice(gl * N_STATE, (gl + 1) * N_STATE)
        srows_im = slice(BLOCK_STATE + gl * N_STATE, BLOCK_STATE + (gl + 1) * N_STATE)
        ecols = slice(BLOCK_IN + gl * GROUP_W, BLOCK_IN + (gl + 1) * GROUP_W)
        cc_ref[j, srows, grows] = crt
        cc_ref[j, srows_im, grows] = -cit
        cc_ref[j, srows, ecols] = crt * arc - cit * aic
        cc_ref[j, srows_im, ecols] = -(cit * arc + crt * aic)
    wb_out_ref[...] = wb_ref[...].astype(jnp.bfloat16)
    cc_out_ref[...] = cc_ref[...].astype(jnp.bfloat16)
    k0_out_ref[...] = k0_ref[...].astype(jnp.bfloat16)


def _discretise(lam_re, lam_im, log_step, b_re, b_im, c_re, c_im):
    f32, bf16 = jnp.float32, jnp.bfloat16
    brt = jnp.transpose(b_re, (0, 2, 1))
    bit = jnp.transpose(b_im, (0, 2, 1))
    crt = jnp.transpose(c_re, (0, 2, 1))
    cit = jnp.transpose(c_im, (0, 2, 1))
    gp = (N_GROUPS, N_STATE)
    wb_shape = (N_CH_BLOCKS, 2 * CH_BLOCK, 2 * GROUPS_PER_CH_BLOCK * N_STATE)
    cc_shape = (N_BLOCKS, 2 * BLOCK_STATE, 2 * BLOCK_IN)
    k0_shape = (N_BLOCKS, BLOCK_IN, BLOCK_IN)
    out_shape = (
        jax.ShapeDtypeStruct(gp, f32), jax.ShapeDtypeStruct(gp, f32), jax.ShapeDtypeStruct(gp, f32),
        jax.ShapeDtypeStruct(wb_shape, bf16), jax.ShapeDtypeStruct(cc_shape, bf16),
        jax.ShapeDtypeStruct(k0_shape, bf16),
    )
    scratch = [pltpu.VMEM(gp, f32) for _ in range(4)] + [pltpu.VMEM((N_STATE, N_GROUPS), f32) for _ in range(2)] + [
        pltpu.VMEM(wb_shape, f32), pltpu.VMEM(cc_shape, f32), pltpu.VMEM(k0_shape, f32)]
    return pl.pallas_call(
        _discretise_kernel,
        out_shape=out_shape,
        scratch_shapes=scratch,
        compiler_params=pltpu.CompilerParams(vmem_limit_bytes=VMEM_LIMIT),
        name="discretise",
    )(lam_re, lam_im, log_step.reshape(N_GROUPS, 1), lam_re.T, lam_im.T, log_step.reshape(1, N_GROUPS),
      brt, bit, c_re, c_im, crt, cit)


def _mixer_kernel(x_ref, h0_ref, c0_ref, g_ref, win_ref, a1_ref, a2_ref, wb_ref, cc_ref, k0_ref, d_ref,
                  wglu_ref, bglu_ref, wconv_ref, bconv_ref,
                  mix_ref, hout_ref, cout_ref,
                  bu_ref, perm_ref, nat_ref, natz_ref, vbuf_ref, xn_ref, *, tt):
    f32, bf16 = jnp.float32, jnp.bfloat16
    nseq = SEQ_PER_STEP
    m = nseq * tt
    m2 = m // 2
    nth = tt // SUBLANES
    half = nseq * QUAD
    ttp = tt + 2 * SUBLANES
    ti = pl.program_id(1)

    @pl.when(ti == 0)
    def _():
        hout_ref[...] = h0_ref[...]
        vbuf_ref[:, 0:SUBLANES, :] = c0_ref[...]
        natz_ref[...] = jnp.zeros(natz_ref.shape, f32)
        for j in range(N_BLOCKS):
            slabs = [h0_ref[j * SLABS + k] for k in range(SLABS)]
            lhs = jnp.concatenate(slabs + [pltpu.roll(sl, nseq, axis=0) for sl in slabs], axis=1).astype(bf16)
            z0 = jnp.dot(lhs, cc_ref[j, :, BLOCK_IN:2 * BLOCK_IN], preferred_element_type=f32)
            for hb in range(2):
                for b in range(nseq):
                    natz_ref[2 * j + hb, pl.ds(b * ttp, 1), :] = z0[b:b + 1, hb * CH_BLOCK:(hb + 1) * CH_BLOCK]

    for hb in range(2):
        rows = slice(hb * m2, (hb + 1) * m2)
        x = x_ref[2 * hb:2 * hb + 2].reshape(m2, D_MODEL)
        r = lax.rsqrt(jnp.mean(x * x, axis=-1, keepdims=True) + EPS)
        xn_ref[rows, :] = ((x * r) * g_ref[...]).astype(bf16)
        u = jnp.dot(xn_ref[rows, :], win_ref[:, 0:D_SSM], preferred_element_type=f32)
        for c in range(N_CH_BLOCKS):
            nat_ref[c, rows, :] = u[:, c * CH_BLOCK:(c + 1) * CH_BLOCK]

    low_rows = lax.broadcasted_iota(jnp.int32, (SUBLANES, LANES), 0) < QUAD

    def halves(first, second, high):
        if high:
            return jnp.where(low_rows, pltpu.roll(first, QUAD, axis=0), second)
        return jnp.where(low_rows, first, pltpu.roll(second, QUAD, axis=0))

    def strided(c, b, th2, parity):
        return pl.ds(b * tt + 2 * SUBLANES * th2 + parity, SUBLANES, stride=2)

    def gather(c, parity):
        out = [None] * (2 * nth)
        for th2 in range(nth // 2):
            for pr in range(nseq // 2):
                e0 = nat_ref[c, strided(c, 2 * pr, th2, parity), :]
                e1 = nat_ref[c, strided(c, 2 * pr + 1, th2, parity), :]
                out[(2 * th2) * 2 + pr] = halves(e0, e1, False)
                out[(2 * th2 + 1) * 2 + pr] = halves(e0, e1, True)
        return jnp.concatenate(out, axis=0)

    sections = 4
    pairs = tt // 2 // sections
    per_gate = D_CONV // GATE_CHUNK
    assert per_gate == N_BLOCKS

    def gate_cols(gate, c):
        start = D_SSM + gate * D_CONV + c * GATE_CHUNK
        return slice(start, start + GATE_CHUNK)

    def project_b(j):
        half_slabs = SLABS // 2
        half_w = GROUPS_PER_CH_BLOCK * N_STATE
        for hb in range(2):
            c = 2 * j + hb
            ccols = slice(c * CH_BLOCK, (c + 1) * CH_BLOCK)
            ev = gather(c, 0)
            od = gather(c, 1)
            perm_ref[0, :, ccols] = ev
            perm_ref[1, :, ccols] = od
            lhs = jnp.concatenate([ev, od], axis=1).astype(bf16)
            v = jnp.dot(lhs, wb_ref[c], preferred_element_type=f32)
            for kk in range(half_slabs):
                k = hb * half_slabs + kk
                re = v[:, kk * LANES:(kk + 1) * LANES]
                im = v[:, half_w + kk * LANES:half_w + (kk + 1) * LANES]
                bu_ref[j % 2, k, :, 0:half, :] = re.reshape(nth, half, LANES)
                bu_ref[j % 2, k, :, half:2 * half, :] = im.reshape(nth, half, LANES)

    def advance(j, q, hs, a1, a2):
        buf = j % 2
        for p in range(q * pairs, (q + 1) * pairs):
            th, p4 = divmod(p, QUAD)
            rows = pl.ds(p4, 2 * nseq, stride=QUAD)
            for k in range(SLABS):
                h = hs[k]
                nh = a1[k] * h + a2[k] * pltpu.roll(h, nseq, axis=0) + bu_ref[buf, k, th, rows, :]
                bu_ref[buf, k, th, rows, :] = nh
                hs[k] = nh
        return hs

    project_b(0)
    for j in range(N_BLOCKS):
        if j + 1 < N_BLOCKS:
            project_b(j + 1)
        a1 = [a1_ref[j * SLABS + k] for k in range(SLABS)]
        a2 = [a2_ref[j * SLABS + k] for k in range(SLABS)]
        hs = [hout_ref[j * SLABS + k] for k in range(SLABS)]

        cols = slice(j * GATE_CHUNK, (j + 1) * GATE_CHUNK)
        gate_b = jnp.dot(xn_ref[...], win_ref[:, gate_cols(0, j)], preferred_element_type=f32)
        hs = advance(j, 0, hs, a1, a2)
        gate_c = jnp.dot(xn_ref[...], win_ref[:, gate_cols(1, j)], preferred_element_type=f32)
        hs = advance(j, 1, hs, a1, a2)
        xv = jnp.dot(xn_ref[...], win_ref[:, gate_cols(2, j)], preferred_element_type=f32)
        hs = advance(j, 2, hs, a1, a2)
        v = gate_c * xv
        vbuf_ref[:, SUBLANES:, cols] = v.reshape(nseq, tt, GATE_CHUNK)
        v1 = vbuf_ref[:, SUBLANES - 1:SUBLANES - 1 + tt, cols].reshape(m, GATE_CHUNK)
        v2 = vbuf_ref[:, SUBLANES - 2:SUBLANES - 2 + tt, cols].reshape(m, GATE_CHUNK)
        yc = (bconv_ref[:, cols] + wconv_ref[0:1, cols] * v2 + wconv_ref[1:2, cols] * v1
              + wconv_ref[2:3, cols] * v)
        mix_ref[:, :, D_SSM + j * GATE_CHUNK:D_SSM + (j + 1) * GATE_CHUNK] = (
            (gate_b * yc).astype(bf16).reshape(nseq, tt, GATE_CHUNK))
        hs = advance(j, 3, hs, a1, a2)

        for k in range(SLABS):
            hout_ref[j * SLABS + k] = hs[k]
        h_all = jnp.concatenate(
            [bu_ref[j % 2, k, :, 0:half, :].reshape(m2, LANES) for k in range(SLABS)]
            + [bu_ref[j % 2, k, :, half:2 * half, :].reshape(m2, LANES) for k in range(SLABS)],
            axis=1).astype(bf16)
        y2 = jnp.dot(h_all, cc_ref[j], preferred_element_type=f32)
        ucols = slice(j * BLOCK_IN, (j + 1) * BLOCK_IN)
        u_even = perm_ref[0, :, ucols]
        u_odd = perm_ref[1, :, ucols]
        feed = jnp.dot(u_even.astype(bf16), k0_ref[j], preferred_element_type=f32)
        perm_ref[0, :, ucols] = feed + d_ref[:, ucols] * u_even
        perm_ref[1, :, ucols] = y2[:, 0:BLOCK_IN] + d_ref[:, ucols] * u_odd
        for hb in range(2):
            zc = y2[:, BLOCK_IN + hb * CH_BLOCK:BLOCK_IN + (hb + 1) * CH_BLOCK]
            for th2 in range(nth // 2):
                for pr in range(nseq // 2):
                    r0 = ((2 * th2) * 2 + pr) * SUBLANES
                    r1 = ((2 * th2 + 1) * 2 + pr) * SUBLANES
                    for odd_seq in (False, True):
                        start = (2 * pr + odd_seq) * ttp + 2 * SUBLANES * th2 + 2
                        natz_ref[2 * j + hb, pl.ds(start, SUBLANES, stride=2), :] = halves(
                            zc[r0:r0 + SUBLANES], zc[r1:r1 + SUBLANES], odd_seq)

    tail = vbuf_ref[:, tt:tt + SUBLANES, :]
    vbuf_ref[:, 0:SUBLANES, :] = tail
    cout_ref[...] = tail

    for c in range(N_CH_BLOCKS):
        ccols = slice(c * CH_BLOCK, (c + 1) * CH_BLOCK)
        for parity in range(2):
            for th2 in range(nth // 2):
                for pr in range(nseq // 2):
                    r0 = ((2 * th2) * 2 + pr) * SUBLANES
                    r1 = ((2 * th2 + 1) * 2 + pr) * SUBLANES
                    first = perm_ref[parity, r0:r0 + SUBLANES, ccols]
                    second = perm_ref[parity, r1:r1 + SUBLANES, ccols]
                    nat_ref[c, strided(c, 2 * pr, th2, parity), :] = halves(first, second, False)
                    nat_ref[c, strided(c, 2 * pr + 1, th2, parity), :] = halves(first, second, True)
    for hb in range(2):
        rows = slice(hb * m2, (hb + 1) * m2)
        y = jnp.concatenate([nat_ref[c, rows, :] for c in range(N_CH_BLOCKS)], axis=1)
        y = y + jnp.concatenate(
            [jnp.concatenate([natz_ref[c, b * ttp:b * ttp + tt, :] for b in (2 * hb, 2 * hb + 1)], axis=0)
             for c in range(N_CH_BLOCKS)], axis=1)
        y = 0.5 * y * (1.0 + lax.erf(y * INV_SQRT2))
        gl = jnp.dot(y.astype(bf16), wglu_ref[...], preferred_element_type=f32) + bglu_ref[...]
        y_ssm = y * jax.nn.sigmoid(gl)
        mix_ref[2 * hb:2 * hb + 2, :, 0:D_SSM] = y_ssm.astype(bf16).reshape(nseq // 2, tt, D_SSM)
    for c in range(N_CH_BLOCKS):
        for b in range(nseq):
            natz_ref[c, pl.ds(b * ttp, 1), :] = natz_ref[c, pl.ds(b * ttp + tt, 1), :]


def _const_spec(shape):
    zeros = (0,) * len(shape)
    return pl.BlockSpec(shape, lambda *_: zeros, pipeline_mode=pl.Buffered(1))


def _mixer(x, h0, c0, norm_g, w_in, a1_tab, a2_tab, wb, cc, k0, d, w_glu, b_glu, w_conv, b_conv, *, tt):
    nbg, nseq, seqlen, _ = x.shape
    assert nseq == SEQ_PER_STEP and seqlen % tt == 0 and tt % (8 * SUBLANES) == 0
    nt = seqlen // tt
    m = nseq * tt
    f32 = jnp.float32
    nstate_rows = N_BLOCKS * SLABS
    in_specs = [
        pl.BlockSpec((None, nseq, tt, D_MODEL), lambda bg, ti: (bg, 0, ti, 0)),
        pl.BlockSpec((None, nstate_rows, SUBLANES, LANES), lambda bg, ti: (bg, 0, 0, 0)),
        pl.BlockSpec((None, nseq, SUBLANES, D_CONV), lambda bg, ti: (bg, 0, 0, 0)),
        _const_spec((1, D_MODEL)),
        _const_spec((D_MODEL, D_IN)),
        _const_spec((nstate_rows, SUBLANES, LANES)),
        _const_spec((nstate_rows, SUBLANES, LANES)),
        _const_spec((N_CH_BLOCKS, 2 * CH_BLOCK, 2 * GROUPS_PER_CH_BLOCK * N_STATE)),
        _const_spec((N_BLOCKS, 2 * BLOCK_STATE, 2 * BLOCK_IN)),
        _const_spec((N_BLOCKS, BLOCK_IN, BLOCK_IN)),
        _const_spec((1, D_SSM)),
        _const_spec((D_SSM, D_SSM)),
        _const_spec((1, D_SSM)),
        _const_spec((CONV_WIDTH, D_CONV)),
        _const_spec((1, D_CONV)),
    ]
    out_specs = [
        pl.BlockSpec((None, nseq, tt, D_MODEL), lambda bg, ti: (bg, 0, ti, 0)),
        pl.BlockSpec((None, nstate_rows, SUBLANES, LANES), lambda bg, ti: (bg, 0, 0, 0)),
        pl.BlockSpec((None, nseq, SUBLANES, D_CONV), lambda bg, ti: (bg, 0, 0, 0)),
    ]
    out_shape = (
        jax.ShapeDtypeStruct((nbg, nseq, seqlen, D_MODEL), jnp.bfloat16),
        jax.ShapeDtypeStruct((nbg, nstate_rows, SUBLANES, LANES), f32),
        jax.ShapeDtypeStruct((nbg, nseq, SUBLANES, D_CONV), f32),
    )
    scratch = [
        pltpu.VMEM((2, SLABS, tt // SUBLANES, QUAD_ROWS, LANES), f32),
        pltpu.VMEM((2, m // 2, D_SSM), f32),
        pltpu.VMEM((N_CH_BLOCKS, m, CH_BLOCK), f32),
        pltpu.VMEM((N_CH_BLOCKS, nseq * (tt + 2 * SUBLANES), CH_BLOCK), f32),
        pltpu.VMEM((nseq, tt + SUBLANES, D_CONV), f32),
        pltpu.VMEM((m, D_MODEL), jnp.bfloat16),
    ]
    return pl.pallas_call(
        functools.partial(_mixer_kernel, tt=tt),
        grid=(nbg, nt),
        in_specs=in_specs,
        out_specs=out_specs,
        out_shape=out_shape,
        scratch_shapes=scratch,
        compiler_params=pltpu.CompilerParams(
            dimension_semantics=("arbitrary", "arbitrary"), vmem_limit_bytes=VMEM_LIMIT),
        name="mixer",
    )(x, h0, c0, norm_g, w_in, a1_tab, a2_tab, wb, cc, k0, d, w_glu, b_glu, w_conv, b_conv)


def _ffn_columns(xn, wup_ref, wdown_ref, acc):
    f32 = jnp.float32
    for hcol in range(wup_ref.shape[1] // MLP_DOT_F):
        cols = slice(hcol * MLP_DOT_F, (hcol + 1) * MLP_DOT_F)
        hm = jnp.dot(xn, wup_ref[:, cols], preferred_element_type=f32)
        act = jnp.square(jnp.maximum(hm, 0.0)).astype(jnp.bfloat16)
        acc = acc + jnp.dot(act, wdown_ref[cols, :], preferred_element_type=f32)
    return acc


def _outproj_kernel(x_ref, mix_ref, wout_ref, g_ref, wup_ref, wdown_ref, acc_ref, xn_ref):
    f32 = jnp.float32
    x1 = x_ref[...] + jnp.dot(mix_ref[...], wout_ref[...], preferred_element_type=f32)
    r = lax.rsqrt(jnp.mean(x1 * x1, axis=-1, keepdims=True) + EPS)
    xn = ((x1 * r) * g_ref[...]).astype(jnp.bfloat16)
    xn_ref[...] = xn
    acc_ref[...] = _ffn_columns(xn, wup_ref, wdown_ref, x1)


def _outproj(x, mix, w_out, norm_g, w_up, w_down, *, tm):
    tokens = x.shape[0]
    tf = w_up.shape[2]
    assert tokens % tm == 0
    row_spec = pl.BlockSpec((tm, D_MODEL), lambda i: (i, 0))
    in_specs = [row_spec, row_spec, _const_spec((D_MODEL, D_MODEL)), _const_spec((1, D_MODEL)),
                pl.BlockSpec((None, D_MODEL, tf), lambda i: (0, 0, 0), pipeline_mode=pl.Buffered(1)),
                pl.BlockSpec((tf, D_MODEL), lambda i: (0, 0), pipeline_mode=pl.Buffered(1))]
    return pl.pallas_call(
        _outproj_kernel,
        grid=(tokens // tm,),
        in_specs=in_specs,
        out_specs=[row_spec, row_spec],
        out_shape=(jax.ShapeDtypeStruct((tokens, D_MODEL), jnp.float32),
                   jax.ShapeDtypeStruct((tokens, D_MODEL), jnp.bfloat16)),
        compiler_params=pltpu.CompilerParams(
            dimension_semantics=("arbitrary",), vmem_limit_bytes=VMEM_LIMIT),
        name="outproj",
    )(x, mix, w_out, norm_g, w_up, w_down)


def _mlp_kernel(x1_ref, xn_ref, wup_ref, wdown_ref, gfin_ref, o_ref):
    f = pl.program_id(1)
    acc = jnp.where(f == 0, x1_ref[...], o_ref[...])
    o_ref[...] = _ffn_columns(xn_ref[...], wup_ref, wdown_ref, acc)

    @pl.when(f == pl.num_programs(1) - 1)
    def _():
        x2 = o_ref[...]
        r = lax.rsqrt(jnp.mean(x2 * x2, axis=-1, keepdims=True) + EPS)
        o_ref[...] = (x2 * r) * gfin_ref[...]


def _mlp(x1, xn, w_up, w_down, norm_final_g, *, tm, first_tile):
    tokens = x1.shape[0]
    tf = w_up.shape[2]
    assert tokens % tm == 0 and w_up.shape == (D_FF // tf, D_MODEL, tf) and tf % MLP_DOT_F == 0
    assert 0 <= first_tile < D_FF // tf
    in_specs = [
        pl.BlockSpec((tm, D_MODEL), lambda i, f: (i, 0)),
        pl.BlockSpec((tm, D_MODEL), lambda i, f: (i, 0)),
        pl.BlockSpec((None, D_MODEL, tf), lambda i, f: (f + first_tile, 0, 0)),
        pl.BlockSpec((tf, D_MODEL), lambda i, f: (f + first_tile, 0)),
        _const_spec((1, D_MODEL)),
    ]
    return pl.pallas_call(
        _mlp_kernel,
        grid=(tokens // tm, D_FF // tf - first_tile),
        in_specs=in_specs,
        out_specs=pl.BlockSpec((tm, D_MODEL), lambda i, f: (i, 0)),
        out_shape=jax.ShapeDtypeStruct((tokens, D_MODEL), jnp.float32),
        compiler_params=pltpu.CompilerParams(
            dimension_semantics=("arbitrary", "arbitrary"), vmem_limit_bytes=VMEM_LIMIT),
        name="mlp",
    )(x1, xn, w_up, w_down, norm_final_g)


def _mlp_cast_kernel(x_ref, mix_ref, wout_ref, g_ref, wup_hbm, wdown_hbm, gfin_ref,
                     o_ref, wup_bf_ref, wdown_bf_ref, xn_ref, upbuf, dnbuf, sems):
    f32, bf16 = jnp.float32, jnp.bfloat16
    f = pl.program_id(0)
    nsteps = pl.num_programs(0)
    tf = upbuf.shape[2]

    def copies(chunk, slot):
        start = chunk * tf if isinstance(chunk, int) else pl.multiple_of(chunk * tf, tf)
        return (pltpu.make_async_copy(wup_hbm.at[:, pl.ds(start, tf)], upbuf.at[slot], sems.at[0, slot]),
                pltpu.make_async_copy(wdown_hbm.at[pl.ds(start, tf), :], dnbuf.at[slot], sems.at[1, slot]))

    @pl.when(f == 0)
    def _():
        for c in range(CAST_RING - 1):
            for cp in copies(c, c):
                cp.start()

    ahead = f + (CAST_RING - 1)

    @pl.when(ahead < nsteps)
    def _():
        for cp in copies(ahead, lax.rem(ahead, CAST_RING)):
            cp.start()

    slot = lax.rem(f, CAST_RING)
    for cp in copies(f, slot):
        cp.wait()
    wup = upbuf[slot].astype(bf16)
    wdown = dnbuf[slot].astype(bf16)
    wup_bf_ref[...] = wup
    wdown_bf_ref[...] = wdown

    @pl.when(f == 0)
    def _():
        x1 = x_ref[...] + jnp.dot(mix_ref[...], wout_ref[...], preferred_element_type=f32)
        o_ref[...] = x1
        r = lax.rsqrt(jnp.mean(x1 * x1, axis=-1, keepdims=True) + EPS)
        xn_ref[...] = ((x1 * r) * g_ref[...]).astype(bf16)

    hm = jnp.dot(xn_ref[...], wup, preferred_element_type=f32)
    act = jnp.square(jnp.maximum(hm, 0.0)).astype(bf16)
    o_ref[...] += jnp.dot(act, wdown, preferred_element_type=f32)

    @pl.when(f == pl.num_programs(0) - 1)
    def _():
        x2 = o_ref[...]
        r = lax.rsqrt(jnp.mean(x2 * x2, axis=-1, keepdims=True) + EPS)
        o_ref[...] = (x2 * r) * gfin_ref[...]


def _mlp_cast(x, mix, w_out, norm_g, w_up_f32, w_down_f32, norm_final_g, *, tf, tf_out):
    tokens = x.shape[0]
    assert D_FF % tf_out == 0 and tf_out % tf == 0
    per_tile = tf_out // tf
    bf16 = jnp.bfloat16
    in_specs = [
        _const_spec((tokens, D_MODEL)),
        _const_spec((tokens, D_MODEL)),
        _const_spec((D_MODEL, D_MODEL)),
        _const_spec((1, D_MODEL)),
        pl.BlockSpec(memory_space=pl.ANY),
        pl.BlockSpec(memory_space=pl.ANY),
        _const_spec((1, D_MODEL)),
    ]
    assert D_FF // tf >= CAST_RING
    out_specs = [
        pl.BlockSpec((tokens, D_MODEL), lambda f: (0, 0)),
        pl.BlockSpec((None, D_MODEL, tf), lambda f: (f // per_tile, 0, f % per_tile)),
        pl.BlockSpec((tf, D_MODEL), lambda f: (f, 0)),
    ]
    out_shape = (
        jax.ShapeDtypeStruct((tokens, D_MODEL), jnp.float32),
        jax.ShapeDtypeStruct((D_FF // tf_out, D_MODEL, tf_out), bf16),
        jax.ShapeDtypeStruct((D_FF, D_MODEL), bf16),
    )
    return pl.pallas_call(
        _mlp_cast_kernel,
        grid=(D_FF // tf,),
        in_specs=in_specs,
        out_specs=out_specs,
        out_shape=out_shape,
        scratch_shapes=[pltpu.VMEM((tokens, D_MODEL), bf16),
                        pltpu.VMEM((CAST_RING, D_MODEL, tf), jnp.float32),
                        pltpu.VMEM((CAST_RING, tf, D_MODEL), jnp.float32),
                        pltpu.SemaphoreType.DMA((2, CAST_RING))],
        compiler_params=pltpu.CompilerParams(
            dimension_semantics=("arbitrary",), vmem_limit_bytes=VMEM_LIMIT),
        name="mlp_cast",
    )(x, mix, w_out, norm_g, w_up_f32, w_down_f32, norm_final_g)


def _state_to_rows(h_re, h_im):
    def one(h):
        nb = h.shape[0]
        h = h.reshape(nb // SEQ_PER_STEP, SEQ_PER_STEP, N_BLOCKS, SLABS, LANES)
        return jnp.transpose(h, (0, 2, 3, 1, 4))
    both = jnp.concatenate([one(h_re), one(h_im)], axis=3)
    return both.reshape(both.shape[0], N_BLOCKS * SLABS, 2 * SEQ_PER_STEP, LANES)


def _rows_to_state(rows):
    nbg = rows.shape[0]
    r = rows.reshape(nbg, N_BLOCKS, SLABS, 2, SEQ_PER_STEP, LANES)
    r = jnp.transpose(r, (3, 0, 4, 1, 2, 5))
    r = r.reshape(2, nbg * SEQ_PER_STEP, N_GROUPS, N_STATE)
    return r[0], r[1]


def _a_tables(a_re, a_im, neg_a_im):
    def rows(t, n):
        return jnp.broadcast_to(t.reshape(N_BLOCKS * SLABS, 1, LANES), (N_BLOCKS * SLABS, n, LANES))
    a1 = rows(a_re, 2 * SEQ_PER_STEP)
    a2 = jnp.concatenate([rows(neg_a_im, SEQ_PER_STEP), rows(a_im, SEQ_PER_STEP)], axis=1)
    return a1, a2


def _run_mixer(x, h_re, h_im, conv_prev, lw, tabs, *, tt):
    bsz, seqlen, _ = x.shape
    nbg = bsz // SEQ_PER_STEP
    x4 = x.reshape(nbg, SEQ_PER_STEP, seqlen, D_MODEL)
    h0 = _state_to_rows(h_re, h_im)
    c0 = jnp.pad(conv_prev, ((0, 0), (SUBLANES - (CONV_WIDTH - 1), 0), (0, 0)))
    c0 = c0.reshape(nbg, SEQ_PER_STEP, SUBLANES, D_CONV)
    mix, hout, cout = _mixer(x4, h0, c0, lw["norm_mix_g"], lw["w_in"], *tabs, lw["d"],
                             lw["w_glu"], lw["b_glu"], lw["w_conv"], lw["b_conv"], tt=tt)
    new_re, new_im = _rows_to_state(hout)
    new_conv = cout.reshape(bsz, SUBLANES, D_CONV)[:, SUBLANES - (CONV_WIDTH - 1):, :]
    return mix.reshape(bsz * seqlen, D_MODEL), new_re, new_im, new_conv


def kernel(x_prompt, x_sample, state_ssm_re, state_ssm_im, cache_conv, norm_mix_g, w_in, ssm_lambda_re, ssm_lambda_im, ssm_log_step, ssm_b_re, ssm_b_im, ssm_c_re, ssm_c_im, ssm_d, w_glu, b_glu, w_conv, b_conv, w_out, norm_mlp_g, w_up, w_down, norm_final_g):
    depth = w_in.shape[0]
    assert depth == 1, "the final norm is fused into the last (only) layer"
    bf16 = jnp.bfloat16
    bp = x_prompt.shape[0]
    a_re, a_im, neg_a_im, wb, cc, k0 = _discretise(
        ssm_lambda_re[0], ssm_lambda_im[0], ssm_log_step[0], ssm_b_re[0], ssm_b_im[0], ssm_c_re[0], ssm_c_im[0])
    tabs = _a_tables(a_re, a_im, neg_a_im) + (wb, cc, k0)
    lw = dict(
        norm_mix_g=norm_mix_g[0].reshape(1, D_MODEL), w_in=w_in[0].astype(bf16),
        d=ssm_d[0].reshape(1, D_SSM), w_glu=w_glu[0].astype(bf16), b_glu=b_glu[0].reshape(1, D_SSM),
        w_conv=w_conv[0], b_conv=b_conv[0].reshape(1, D_CONV), w_out=w_out[0].astype(bf16),
        norm_mlp_g=norm_mlp_g[0].reshape(1, D_MODEL))
    gfin = norm_final_g.reshape(1, D_MODEL)

    zeros_h = jnp.zeros((bp, N_GROUPS, N_STATE), jnp.float32)
    zeros_c = jnp.zeros((bp, CONV_WIDTH - 1, D_CONV), x_prompt.dtype)
    mix_s, sre, sim, scv = _run_mixer(x_sample, state_ssm_re[0], state_ssm_im[0], cache_conv[0], lw, tabs, tt=SAMPLE_TILE_T)
    mix_p, pre, pim, pcv = _run_mixer(x_prompt, zeros_h, zeros_h, zeros_c, lw, tabs, tt=PROMPT_TILE_T)
    ys, w_up_bf, w_down_bf = _mlp_cast(x_sample.reshape(-1, D_MODEL), mix_s, lw["w_out"], lw["norm_mlp_g"],
                                       w_up.reshape(D_MODEL, D_FF), w_down.reshape(D_FF, D_MODEL), gfin,
                                       tf=CAST_TILE_F, tf_out=MLP_TILE_F)
    acc_p, xn_p = _outproj(x_prompt.reshape(-1, D_MODEL), mix_p, lw["w_out"], lw["norm_mlp_g"], w_up_bf, w_down_bf,
                           tm=MLP_TILE_M)
    yp = _mlp(acc_p, xn_p, w_up_bf, w_down_bf, gfin, tm=MLP_TILE_M, first_tile=1)
    return (yp.reshape(x_prompt.shape), ys.reshape(x_sample.shape), pre[None], pim[None], pcv[None],
            sre[None], sim[None], scv[None])
```
